```python
import jax, jax.numpy as jnp
from jax import lax
import numpy as np

D_MODEL = 2048
BATCH = 2
SEQ = 4096
DEPTH = 1

GDN_HEADS = 8
GDN_DK = 128
GDN_DV = 128
GDN_KEY = GDN_HEADS * GDN_DK
GDN_VAL = GDN_HEADS * GDN_DV
MLSTM_HEADS = 8
MLSTM_DH = 128
MLSTM_W = MLSTM_HEADS * MLSTM_DH
N_DIR = 2
N_BRANCH = 2
CONV_WIDTH = 5
CHUNK = 64
NORM_EPS = 1e-6
IN_SIZES = (
    2 * GDN_KEY + GDN_VAL,
    GDN_VAL,
    N_DIR * GDN_HEADS,
    N_DIR * GDN_HEADS,
    2 * MLSTM_W,
    MLSTM_W,
    MLSTM_W,
    MLSTM_W,
    N_DIR * MLSTM_HEADS,
    N_DIR * MLSTM_HEADS,
    N_BRANCH * D_MODEL,
)
IN_COLS = sum(IN_SIZES)

kernel_name = "bidir_gdn_mlstm_gated_hybrid"


def rmsnorm(x, w):
    xf = x.astype(jnp.float32)
    y = xf * lax.rsqrt(jnp.mean(xf * xf, axis=-1, keepdims=True) + NORM_EPS)
    return (y * w.astype(jnp.float32)).astype(x.dtype)


def l2norm(x):
    return x * lax.rsqrt(jnp.sum(x * x, axis=-1, keepdims=True) + NORM_EPS)


def centred_depthwise_conv(x, w):
    pad = (w.shape[0] - 1) // 2
    return lax.conv_general_dilated(
        x, w[:, None, :].astype(x.dtype), window_strides=(1,), padding=[(pad, pad)],
        dimension_numbers=("NWC", "WIO", "NWC"), feature_group_count=x.shape[-1])


def to_heads(t, n_heads):
    b, s, c = t.shape
    return t.reshape(b, s, n_heads, c // n_heads).transpose(0, 2, 1, 3)


def gate_heads(t, n_heads):
    b, s, _ = t.shape
    return t.reshape(b, s, N_DIR, n_heads).transpose(2, 0, 3, 1)


def to_chunks(t):
    b, h, s = t.shape[:3]
    return t.reshape((b, h, s // CHUNK, CHUNK) + t.shape[3:])


def gated_delta_chunked(q, k, v, g, beta):
    b_, h_, s_, dk = q.shape
    dv = v.shape[-1]
    q, k, v = to_chunks(q), to_chunks(k), to_chunks(v)
    g, beta = to_chunks(g), to_chunks(beta)
    gc = jnp.cumsum(g, axis=-1)
    causal = jnp.tril(jnp.ones((CHUNK, CHUNK), dtype=bool))
    strict = jnp.tril(jnp.ones((CHUNK, CHUNK), dtype=bool), k=-1)
    decay = jnp.exp(jnp.where(causal, gc[..., :, None] - gc[..., None, :], -jnp.inf))
    kb = k * beta[..., None]
    lower = jnp.where(strict, jnp.einsum("bhnik,bhnjk->bhnij", kb, k) * decay, 0.0)
    a_mat = lower + jnp.eye(CHUNK, dtype=q.dtype)
    rhs = jnp.concatenate([v * beta[..., None], kb * jnp.exp(gc)[..., None]], axis=-1)
    sol = lax.linalg.triangular_solve(a_mat, rhs, left_side=True, lower=True, unit_diagonal=True)
    u, w = sol[..., :dv], sol[..., dv:]
    attn = jnp.einsum("bhnik,bhnjk->bhnij", q, k) * decay
    g_last = gc[..., -1]
    k_dec = k * jnp.exp(g_last[..., None] - gc)[..., None]

    def step(state, inp):
        u_c, w_c, kd_c, gl_c = inp
        v_new = u_c - jnp.einsum("bhlk,bhkv->bhlv", w_c, state)
        state_next = state * jnp.exp(gl_c)[..., None, None] + jnp.einsum("bhlk,bhlv->bhkv", kd_c, v_new)
        return state_next, (state, v_new)

    init = jnp.zeros((b_, h_, dk, dv), q.dtype)
    xs = tuple(jnp.moveaxis(t, 2, 0) for t in (u, w, k_dec, g_last))
    _, (s_all, v_new) = lax.scan(step, init, xs)
    s_all = jnp.moveaxis(s_all, 0, 2)
    v_new = jnp.moveaxis(v_new, 0, 2)
    out = (jnp.einsum("bhnlk,bhnkv->bhnlv", q * jnp.exp(gc)[..., None], s_all)
           + jnp.einsum("bhnij,bhnjv->bhniv", attn, v_new))
    return out.reshape(b_, h_, s_, dv)


def mlstm_chunked(q, k, v, i_pre, f_pre):
    b_, h_, s_, dk = q.shape
    dv = v.shape[-1]
    q, k, v = to_chunks(q), to_chunks(k), to_chunks(v)
    ig = to_chunks(i_pre)
    bcum = jnp.cumsum(jax.nn.log_sigmoid(to_chunks(f_pre)), axis=-1)
    causal = jnp.tril(jnp.ones((CHUNK, CHUNK), dtype=bool))
    d_log = jnp.where(causal, bcum[..., :, None] - bcum[..., None, :] + ig[..., None, :], -jnp.inf)
    m_intra = jnp.max(d_log, axis=-1)
    a_end = bcum[..., -1:] - bcum + ig
    m_loc = jnp.max(a_end, axis=-1)
    k_end = k * jnp.exp(a_end - m_loc[..., None])[..., None]
    d_c = jnp.einsum("bhnlk,bhnlv->bhnkv", k_end, v)
    d_n = jnp.sum(k_end, axis=-2)
    b_last = bcum[..., -1]

    def step(carry, inp):
        c_st, n_st, m_st = carry
        dc_c, dn_c, ml_c, bl_c = inp
        m_new = jnp.maximum(bl_c + m_st, ml_c)
        s_old = jnp.exp(bl_c + m_st - m_new)
        s_loc = jnp.exp(ml_c - m_new)
        c_next = c_st * s_old[..., None, None] + dc_c * s_loc[..., None, None]
        n_next = n_st * s_old[..., None] + dn_c * s_loc[..., None]
        return (c_next, n_next, m_new), (c_st, n_st, m_st)

    init = (jnp.zeros((b_, h_, dk, dv), q.dtype), jnp.zeros((b_, h_, dk), q.dtype),
            jnp.zeros((b_, h_), q.dtype))
    xs = tuple(jnp.moveaxis(t, 2, 0) for t in (d_c, d_n, m_loc, b_last))
    _, (c_all, n_all, m_all) = lax.scan(step, init, xs)
    c_all = jnp.moveaxis(c_all, 0, 2)
    n_all = jnp.moveaxis(n_all, 0, 2)
    m_all = jnp.moveaxis(m_all, 0, 2)
    m_inter = bcum + m_all[..., None]
    m_t = jnp.maximum(m_inter, m_intra)
    w_inter = jnp.exp(m_inter - m_t)
    scores = jnp.einsum("bhnik,bhnjk->bhnij", q, k) * jnp.exp(d_log - m_t[..., None])
    num = (w_inter[..., None] * jnp.einsum("bhnlk,bhnkv->bhnlv", q, c_all)
           + jnp.einsum("bhnij,bhnjv->bhniv", scores, v))
    den = w_inter * jnp.einsum("bhnlk,bhnk->bhnl", q, n_all) + jnp.sum(scores, axis=-1)
    h = num / jnp.maximum(jnp.abs(den), jnp.exp(-m_t))[..., None]
    return h.reshape(b_, h_, s_, dv)


def flip_t(t, axis):
    return jnp.flip(t, axis=axis)


def gdn_branch(qkv, z, beta_pre, a_pre, conv_w, a_log, dt_bias, norm_w):
    f32 = jnp.float32
    qkv = jax.nn.silu(centred_depthwise_conv(qkv, conv_w)).astype(f32)
    q = l2norm(to_heads(qkv[..., :GDN_KEY], GDN_HEADS)) * (GDN_DK ** -0.5)
    k = l2norm(to_heads(qkv[..., GDN_KEY:2 * GDN_KEY], GDN_HEADS))
    v = to_heads(qkv[..., 2 * GDN_KEY:], GDN_HEADS)
    beta = jax.nn.sigmoid(gate_heads(beta_pre.astype(f32), GDN_HEADS))
    g = (-jnp.exp(a_log.astype(f32))[:, None, :, None]
         * jax.nn.softplus(gate_heads(a_pre.astype(f32), GDN_HEADS) + dt_bias.astype(f32)[:, None, :, None]))
    o_fwd = gated_delta_chunked(q, k, v, g[0], beta[0])
    o_bwd = flip_t(gated_delta_chunked(flip_t(q, 2), flip_t(k, 2), flip_t(v, 2),
                                       flip_t(g[1], 2), flip_t(beta[1], 2)), 2)
    o = (o_fwd + o_bwd).transpose(0, 2, 1, 3)
    o = o * lax.rsqrt(jnp.mean(o * o, axis=-1, keepdims=True) + NORM_EPS) * norm_w.astype(f32)
    o = o.reshape(o.shape[0], o.shape[1], GDN_VAL) * jax.nn.silu(z.astype(f32))
    return o.astype(z.dtype)


def mlstm_branch(qk, v, o_pre, z, i_pre, f_pre, conv_w, i_bias, f_bias, norm_w):
    f32 = jnp.float32
    qk = jax.nn.silu(centred_depthwise_conv(qk, conv_w)).astype(f32)
    q = to_heads(qk[..., :MLSTM_W], MLSTM_HEADS)
    k = to_heads(qk[..., MLSTM_W:], MLSTM_HEADS) * (MLSTM_DH ** -0.5)
    v = to_heads(v.astype(f32), MLSTM_HEADS)
    ig = gate_heads(i_pre.astype(f32), MLSTM_HEADS) + i_bias.astype(f32)[:, None, :, None]
    fg = gate_heads(f_pre.astype(f32), MLSTM_HEADS) + f_bias.astype(f32)[:, None, :, None]
    h_fwd = mlstm_chunked(q, k, v, ig[0], fg[0])
    h_bwd = flip_t(mlstm_chunked(flip_t(q, 2), flip_t(k, 2), flip_t(v, 2),
                                 flip_t(ig[1], 2), flip_t(fg[1], 2)), 2)
    h = (h_fwd + h_bwd).transpose(0, 2, 1, 3)
    mu = jnp.mean(h, axis=-1, keepdims=True)
    hc = h - mu
    h = hc * lax.rsqrt(jnp.mean(hc * hc, axis=-1, keepdims=True) + NORM_EPS)
    h = h.reshape(h.shape[0], h.shape[1], MLSTM_W) * norm_w.astype(f32)
    h = h * jax.nn.sigmoid(o_pre.astype(f32)) * jax.nn.silu(z.astype(f32))
    return h.astype(z.dtype)


def setup_inputs(seed: int = 0) -> dict:
    key = jax.random.key(seed)
    ks = jax.random.split(key, 20)
    f32 = jnp.float32
    nrm = lambda k, shape, scale: jax.random.normal(k, shape, f32) * scale
    x = jax.random.normal(ks[0], (BATCH, SEQ, D_MODEL), f32)
    w_in = nrm(ks[1], (DEPTH, D_MODEL, IN_COLS), D_MODEL ** -0.5)
    conv_gdn = nrm(ks[2], (DEPTH, CONV_WIDTH, 2 * GDN_KEY + GDN_VAL), CONV_WIDTH ** -0.5)
    gdn_a_log = jnp.log(jax.random.uniform(ks[3], (DEPTH, N_DIR, GDN_HEADS), f32, 1.0, 16.0))
    dt = jnp.exp(jax.random.uniform(ks[4], (DEPTH, N_DIR, GDN_HEADS), f32, np.log(1e-3), np.log(1e-1)))
    gdn_dt_bias = dt + jnp.log(-jnp.expm1(-dt))
    gdn_norm_w = 1.0 + nrm(ks[5], (DEPTH, GDN_DV), 0.02)
    conv_mlstm = nrm(ks[6], (DEPTH, CONV_WIDTH, 2 * MLSTM_W), CONV_WIDTH ** -0.5)
    mlstm_i_bias = nrm(ks[7], (DEPTH, N_DIR, MLSTM_HEADS), 0.1)
    mlstm_f_bias = (jnp.broadcast_to(jnp.linspace(3.0, 6.0, MLSTM_HEADS, dtype=f32), (DEPTH, N_DIR, MLSTM_HEADS))
                    + nrm(ks[8], (DEPTH, N_DIR, MLSTM_HEADS), 0.1))
    mlstm_norm_w = 1.0 + nrm(ks[9], (DEPTH, MLSTM_W), 0.02)
    gate_bias = nrm(ks[10], (DEPTH, N_BRANCH * D_MODEL), 0.02)
    w_branch_gdn = nrm(ks[11], (DEPTH, GDN_VAL, D_MODEL), GDN_VAL ** -0.5)
    w_branch_mlstm = nrm(ks[12], (DEPTH, MLSTM_W, D_MODEL), MLSTM_W ** -0.5)
    w_out = nrm(ks[13], (DEPTH, D_MODEL, D_MODEL), D_MODEL ** -0.5)
    norm_w = 1.0 + nrm(ks[14], (DEPTH, D_MODEL), 0.02)
    final_norm_w = 1.0 + nrm(ks[15], (D_MODEL,), 0.02)
    return {"x": x, "w_in": w_in, "conv_gdn": conv_gdn, "gdn_a_log": gdn_a_log,
            "gdn_dt_bias": gdn_dt_bias, "gdn_norm_w": gdn_norm_w, "conv_mlstm": conv_mlstm,
            "mlstm_i_bias": mlstm_i_bias, "mlstm_f_bias": mlstm_f_bias, "mlstm_norm_w": mlstm_norm_w,
            "gate_bias": gate_bias, "w_branch_gdn": w_branch_gdn, "w_branch_mlstm": w_branch_mlstm,
            "w_out": w_out, "norm_w": norm_w, "final_norm_w": final_norm_w}


def reference(x, w_in, conv_gdn, gdn_a_log, gdn_dt_bias, gdn_norm_w, conv_mlstm, mlstm_i_bias,
              mlstm_f_bias, mlstm_norm_w, gate_bias, w_branch_gdn, w_branch_mlstm, w_out, norm_w,
              final_norm_w):
    split_idx = [int(i) for i in np.cumsum(IN_SIZES)[:-1]]
    b_, s_, _ = x.shape
    for layer in range(DEPTH):
        n = rmsnorm(x, norm_w[layer])
        proj = jnp.einsum("btd,dc->btc", n, w_in[layer])
        (g_qkv, g_z, g_beta, g_a, m_qk, m_v, m_o, m_z, m_i, m_f, gates) = jnp.split(proj, split_idx, axis=-1)
        y_a = gdn_branch(g_qkv, g_z, g_beta, g_a, conv_gdn[layer], gdn_a_log[layer],
                         gdn_dt_bias[layer], gdn_norm_w[layer])
        y_b = mlstm_branch(m_qk, m_v, m_o, m_z, m_i, m_f, conv_mlstm[layer], mlstm_i_bias[layer],
                           mlstm_f_bias[layer], mlstm_norm_w[layer])
        gates = jax.nn.sigmoid(gates + gate_bias[layer]).reshape(b_, s_, N_BRANCH, D_MODEL)
        merged = (gates[:, :, 0] * jnp.einsum("btc,cd->btd", y_a, w_branch_gdn[layer])
                  + gates[:, :, 1] * jnp.einsum("btc,cd->btd", y_b, w_branch_mlstm[layer]))
        x = x + jnp.einsum("btd,de->bte", merged, w_out[layer])
    return rmsnorm(x, final_norm_w)
```

```python
import functools

import jax
import jax.numpy as jnp
from jax import lax
from jax.experimental import pallas as pl
from jax.experimental.pallas import tpu as pltpu

F32 = jnp.float32
BF16 = jnp.bfloat16
HIGHEST = lax.Precision.HIGHEST

HEADS = 8
HEAD_DIM = 128
CHUNK = 64
CONV_WIDTH = 5
N_DIR = 2
NORM_EPS = 1e-6
LANES = 128
CONV_ROWS = 256
CONV_HALO = 8
VMEM_LIMIT = 56 * 1024 * 1024

COL_GATES = 0
COL_GQ, COL_GK, COL_GV, COL_GZ = 32, 40, 48, 56
COL_MQ, COL_MK, COL_MV, COL_MO, COL_MZ = 64, 72, 80, 88, 96
BIG_COLS = 104 * LANES


def _mm(a, b):
    return jnp.dot(a.astype(BF16), b.astype(BF16), preferred_element_type=F32)


def _mm_nt(a, b):
    return lax.dot_general(a.astype(BF16), b.astype(BF16), (((1,), (1,)), ((), ())),
                           preferred_element_type=F32)


def _mm_tn(a, b):
    return jnp.dot(a.T.astype(BF16), b.astype(BF16), preferred_element_type=F32)


def _silu(x):
    return x * jax.nn.sigmoid(x)


def _in_proj_kernel(x_ref, nw_ref, wbig_ref, wsmall_ref, big_ref, small_ref, n_scr):
    @pl.when(pl.program_id(1) == 0)
    def _():
        x = x_ref[...]
        y = x * lax.rsqrt(jnp.mean(x * x, axis=-1, keepdims=True) + NORM_EPS) * nw_ref[...]
        n_scr[...] = y.astype(BF16)
        small_ref[...] = jnp.dot(y, wsmall_ref[...], precision=HIGHEST, preferred_element_type=F32)

    big_ref[...] = jnp.dot(n_scr[...], wbig_ref[...], preferred_element_type=F32)


def _in_proj(x2, norm_w, w_big, w_small):
    m, d = x2.shape
    tm = min(1024, m)
    tn = 1024
    return pl.pallas_call(
        _in_proj_kernel,
        grid=(m // tm, BIG_COLS // tn),
        in_specs=[
            pl.BlockSpec((tm, d), lambda i, j: (i, 0)),
            pl.BlockSpec((1, d), lambda i, j: (0, 0)),
            pl.BlockSpec((d, tn), lambda i, j: (0, j)),
            pl.BlockSpec((d, LANES), lambda i, j: (0, 0)),
        ],
        out_specs=[
            pl.BlockSpec((tm, tn), lambda i, j: (i, j)),
            pl.BlockSpec((tm, LANES), lambda i, j: (i, 0)),
        ],
        out_shape=[
            jax.ShapeDtypeStruct((m, BIG_COLS), F32),
            jax.ShapeDtypeStruct((m, LANES), F32),
        ],
        scratch_shapes=[pltpu.VMEM((tm, d), BF16)],
        compiler_params=pltpu.CompilerParams(
            dimension_semantics=("parallel", "arbitrary"), vmem_limit_bytes=VMEM_LIMIT),
        name="in_proj",
    )(x2, norm_w, w_big, w_small)


def _conv_silu_into(src_ref, w_ref, pad_scr, dst_scr, post):
    t = src_ref.shape[0]
    rb = min(CONV_ROWS, t)
    win_rows = rb + 2 * CONV_HALO
    zeros = jnp.zeros((CONV_HALO, HEAD_DIM), F32)
    pad_scr[0:CONV_HALO, :] = zeros
    pad_scr[t + CONV_HALO:t + 2 * CONV_HALO, :] = zeros

    def copy(r, carry):
        off = pl.multiple_of(r * rb, rb)
        pad_scr[pl.ds(off + CONV_HALO, rb), :] = src_ref[pl.ds(off, rb), :]
        return carry

    lax.fori_loop(0, t // rb, copy, 0)
    w = w_ref[...]
    pad = (CONV_WIDTH - 1) // 2

    def body(r, carry):
        off = pl.multiple_of(r * rb, rb)
        win = pad_scr[pl.ds(off, win_rows), :]
        acc = None
        for j in range(CONV_WIDTH):
            shift = (pad - j) % win_rows
            rolled = pltpu.roll(win, shift, 0) if shift else win
            term = rolled[CONV_HALO:CONV_HALO + rb, :] * w[j:j + 1, :]
            acc = term if acc is None else acc + term
        dst_scr[pl.ds(off, rb), :] = post(_silu(acc))
        return carry

    lax.fori_loop(0, t // rb, body, 0)


def _direction_masks():
    row = lax.broadcasted_iota(jnp.int32, (CHUNK, CHUNK), 0)
    col = lax.broadcasted_iota(jnp.int32, (CHUNK, CHUNK), 1)
    fwd = (row >= col, row > col)
    bwd = (row <= col, row < col)
    out = []
    for (incl, strict), last in ((fwd, CHUNK - 1), (bwd, 0)):
        out.append((incl, strict, incl.astype(F32), last))
    return out


def _chunk_cumsum(tri, col):
    return jnp.dot(tri, jnp.broadcast_to(col, (CHUNK, LANES)), precision=HIGHEST,
                   preferred_element_type=F32)


def _as_rows(col_bcast):
    return col_bcast.T[:CHUNK, :]


def _unit_triangular_inverse(lm, eye):
    p = eye - lm
    power = lm
    exponent = 2
    while exponent < CHUNK:
        power = _mm(power, power)
        p = p + _mm(p, power)
        exponent *= 2
    return p


def _gdn_kernel(alog_ref, dtb_ref, q_ref, k_ref, v_ref, z_ref, g_ref, cq_ref, ck_ref, cv_ref, nw_ref,
                y_ref, pad_scr, q_scr, k_scr, v_scr, o_scr, sf_scr, sb_scr):
    h = pl.program_id(1)
    t = q_ref.shape[0]
    n_chunks = t // CHUNK

    def l2(y):
        return y * lax.rsqrt(jnp.sum(y * y, axis=-1, keepdims=True) + NORM_EPS)

    _conv_silu_into(q_ref, cq_ref, pad_scr, q_scr, lambda y: l2(y) * (HEAD_DIM ** -0.5))
    _conv_silu_into(k_ref, ck_ref, pad_scr, k_scr, l2)
    _conv_silu_into(v_ref, cv_ref, pad_scr, v_scr, lambda y: y)

    o_scr[...] = jnp.zeros_like(o_scr)
    sf_scr[...] = jnp.zeros_like(sf_scr)
    sb_scr[...] = jnp.zeros_like(sb_scr)

    masks = _direction_masks()
    eye = (lax.broadcasted_iota(jnp.int32, (CHUNK, CHUNK), 0)
           == lax.broadcasted_iota(jnp.int32, (CHUNK, CHUNK), 1)).astype(F32)

    def one_chunk(c, d, s_ref):
        incl, strict, tri, last = masks[d]
        off = pl.multiple_of(c * CHUNK, CHUNK)
        qc = q_scr[pl.ds(off, CHUNK), :]
        kc = k_scr[pl.ds(off, CHUNK), :]
        vc = v_scr[pl.ds(off, CHUNK), :]
        gt = g_ref[pl.ds(off, CHUNK), :]
        beta = jax.nn.sigmoid(gt[:, d:d + 1])
        a_log = jnp.full((CHUNK, 1), alog_ref[d, h], F32)
        dt_bias = jnp.full((CHUNK, 1), dtb_ref[d, h], F32)
        g = -jnp.exp(a_log) * jax.nn.softplus(gt[:, 2 + d:3 + d] + dt_bias)
        gcb = _chunk_cumsum(tri, g)
        decay = jnp.exp(jnp.where(incl, gcb[:, :CHUNK] - _as_rows(gcb), -jnp.inf))
        kb = kc * beta
        lm = jnp.where(strict, _mm_nt(kb, kc) * decay, 0.0)
        egc = jnp.exp(gcb)
        rhs = jnp.concatenate([vc * beta, kb * egc], axis=1)
        sol = _mm(_unit_triangular_inverse(lm, eye), rhs)
        u = sol[:, :HEAD_DIM]
        w = sol[:, HEAD_DIM:]
        attn = _mm_nt(qc, kc) * decay
        g_last = gcb[last:last + 1, :]
        k_dec = kc * jnp.exp(g_last - gcb)
        state = s_ref[...]
        v_new = u - _mm(w, state)
        out = _mm(qc * egc, state) + _mm(attn, v_new)
        s_ref[...] = state * jnp.exp(g_last) + _mm_tn(k_dec, v_new)
        o_scr[pl.ds(off, CHUNK), :] += out

    def step(s, carry):
        one_chunk(s, 0, sf_scr)
        one_chunk(n_chunks - 1 - s, 1, sb_scr)
        return carry

    lax.fori_loop(0, n_chunks, step, 0)

    rb = min(CONV_ROWS, t)
    nw = nw_ref[...]

    def epilogue(r, carry):
        off = pl.multiple_of(r * rb, rb)
        o = o_scr[pl.ds(off, rb), :]
        o = o * lax.rsqrt(jnp.mean(o * o, axis=-1, keepdims=True) + NORM_EPS) * nw
        y_ref[pl.ds(off, rb), :] = (o * _silu(z_ref[pl.ds(off, rb), :])).astype(y_ref.dtype)
        return carry

    lax.fori_loop(0, t // rb, epilogue, 0)


def _gdn(proj_big, gsm, conv_w, a_log, dt_bias, norm_w, batch, t):
    head_block = lambda base: pl.BlockSpec((t, HEAD_DIM), lambda b, h: (b, base + h))
    conv_block = lambda base: pl.BlockSpec((CONV_WIDTH, HEAD_DIM), lambda b, h: (0, base + h))
    smem = pl.BlockSpec(memory_space=pltpu.SMEM)
    return pl.pallas_call(
        _gdn_kernel,
        grid=(batch, HEADS),
        in_specs=[
            smem, smem,
            head_block(COL_GQ), head_block(COL_GK), head_block(COL_GV), head_block(COL_GZ),
            pl.BlockSpec((None, None, t, 4 * N_DIR), lambda b, h: (b, h, 0, 0)),
            conv_block(0), conv_block(HEADS), conv_block(2 * HEADS),
            pl.BlockSpec((1, HEAD_DIM), lambda b, h: (0, 0)),
        ],
        out_specs=pl.BlockSpec((t, HEAD_DIM), lambda b, h: (b, h)),
        out_shape=jax.ShapeDtypeStruct((batch * t, HEADS * HEAD_DIM), BF16),
        scratch_shapes=[
            pltpu.VMEM((t + 2 * CONV_HALO, HEAD_DIM), F32),
            pltpu.VMEM((t, HEAD_DIM), F32),
            pltpu.VMEM((t, HEAD_DIM), F32),
            pltpu.VMEM((t, HEAD_DIM), F32),
            pltpu.VMEM((t, HEAD_DIM), F32),
            pltpu.VMEM((HEAD_DIM, HEAD_DIM), F32),
            pltpu.VMEM((HEAD_DIM, HEAD_DIM), F32),
        ],
        compiler_params=pltpu.CompilerParams(
            dimension_semantics=("parallel", "parallel"), vmem_limit_bytes=VMEM_LIMIT),
        name="gdn",
    )(a_log, dt_bias, proj_big, proj_big, proj_big, proj_big, gsm, conv_w, conv_w, conv_w, norm_w)


def _mlstm_kernel(ib_ref, fb_ref, q_ref, k_ref, v_ref, op_ref, z_ref, g_ref, cq_ref, ck_ref, nw_ref,
                  y_ref, pad_scr, q_scr, k_scr, o_scr, cf_scr, cb_scr, mf_scr, mb_scr):
    h = pl.program_id(1)
    t = q_ref.shape[0]
    n_chunks = t // CHUNK

    _conv_silu_into(q_ref, cq_ref, pad_scr, q_scr, lambda y: y)
    _conv_silu_into(k_ref, ck_ref, pad_scr, k_scr, lambda y: y * (HEAD_DIM ** -0.5))

    o_scr[...] = jnp.zeros_like(o_scr)
    cf_scr[...] = jnp.zeros_like(cf_scr)
    cb_scr[...] = jnp.zeros_like(cb_scr)
    mf_scr[...] = jnp.zeros_like(mf_scr)
    mb_scr[...] = jnp.zeros_like(mb_scr)

    masks = _direction_masks()
    ones_col = (lax.broadcasted_iota(jnp.int32, (CHUNK, LANES), 1) == 0).astype(F32)

    def both(x):
        return jnp.concatenate([x, x], axis=1)

    def one_chunk(c, d, c_ref, m_ref):
        incl, _, tri, last = masks[d]
        off = pl.multiple_of(c * CHUNK, CHUNK)
        qc = q_scr[pl.ds(off, CHUNK), :]
        kc = k_scr[pl.ds(off, CHUNK), :]
        vc = v_ref[pl.ds(off, CHUNK), :]
        gt = g_ref[pl.ds(off, CHUNK), :]
        i_bias = jnp.full((CHUNK, 1), ib_ref[d, h], F32)
        f_bias = jnp.full((CHUNK, 1), fb_ref[d, h], F32)
        igb = jnp.broadcast_to(gt[:, 4 + d:5 + d] + i_bias, (CHUNK, LANES))
        bcb = _chunk_cumsum(tri, jax.nn.log_sigmoid(gt[:, 6 + d:7 + d] + f_bias))
        d_log = jnp.where(incl, bcb[:, :CHUNK] - _as_rows(bcb) + _as_rows(igb), -jnp.inf)
        m_intra = jnp.max(d_log, axis=1, keepdims=True)
        b_last = bcb[last:last + 1, :]
        a_end = b_last - bcb + igb
        m_loc = jnp.max(a_end, axis=0, keepdims=True)
        k_end = kc * jnp.exp(a_end - m_loc)
        v_aug = jnp.concatenate([vc, ones_col], axis=1)
        delta = _mm_tn(k_end, v_aug)
        c_aug = c_ref[...]
        m_st = m_ref[...]
        m_inter = bcb + m_st
        m_t = jnp.maximum(m_inter, m_intra)
        w_inter = jnp.exp(m_inter - m_t)
        scores = _mm_nt(qc, kc) * jnp.exp(d_log - m_t[:, :CHUNK])
        num_aug = both(w_inter) * _mm(qc, c_aug) + _mm(scores, v_aug)
        num = num_aug[:, :HEAD_DIM]
        den = num_aug[:, HEAD_DIM:HEAD_DIM + 1]
        h_out = num / jnp.maximum(jnp.abs(den), jnp.exp(-m_t))
        m_new = jnp.maximum(b_last + m_st, m_loc)
        s_old = jnp.exp(b_last + m_st - m_new)
        s_loc = jnp.exp(m_loc - m_new)
        c_ref[...] = c_aug * both(s_old) + delta * both(s_loc)
        m_ref[...] = m_new
        o_scr[pl.ds(off, CHUNK), :] += h_out

    def step(s, carry):
        one_chunk(s, 0, cf_scr, mf_scr)
        one_chunk(n_chunks - 1 - s, 1, cb_scr, mb_scr)
        return carry

    lax.fori_loop(0, n_chunks, step, 0)

    rb = min(CONV_ROWS, t)
    nw = nw_ref[...]

    def epilogue(r, carry):
        off = pl.multiple_of(r * rb, rb)
        hh = o_scr[pl.ds(off, rb), :]
        hc = hh - jnp.mean(hh, axis=-1, keepdims=True)
        hn = hc * lax.rsqrt(jnp.mean(hc * hc, axis=-1, keepdims=True) + NORM_EPS) * nw
        gate = jax.nn.sigmoid(op_ref[pl.ds(off, rb), :]) * _silu(z_ref[pl.ds(off, rb), :])
        y_ref[pl.ds(off, rb), :] = (hn * gate).astype(y_ref.dtype)
        return carry

    lax.fori_loop(0, t // rb, epilogue, 0)


def _mlstm(proj_big, gsm, conv_w, i_bias, f_bias, norm_w, batch, t):
    head_block = lambda base: pl.BlockSpec((t, HEAD_DIM), lambda b, h: (b, base + h))
    conv_block = lambda base: pl.BlockSpec((CONV_WIDTH, HEAD_DIM), lambda b, h: (0, base + h))
    smem = pl.BlockSpec(memory_space=pltpu.SMEM)
    return pl.pallas_call(
        _mlstm_kernel,
        grid=(batch, HEADS),
        in_specs=[
            smem, smem,
            head_block(COL_MQ), head_block(COL_MK), head_block(COL_MV), head_block(COL_MO),
            head_block(COL_MZ),
            pl.BlockSpec((None, None, t, 4 * N_DIR), lambda b, h: (b, h, 0, 0)),
            conv_block(0), conv_block(HEADS),
            pl.BlockSpec((1, HEAD_DIM), lambda b, h: (0, h)),
        ],
        out_specs=pl.BlockSpec((t, HEAD_DIM), lambda b, h: (b, h)),
        out_shape=jax.ShapeDtypeStruct((batch * t, HEADS * HEAD_DIM), BF16),
        scratch_shapes=[
            pltpu.VMEM((t + 2 * CONV_HALO, HEAD_DIM), F32),
            pltpu.VMEM((t, HEAD_DIM), F32),
            pltpu.VMEM((t, HEAD_DIM), F32),
            pltpu.VMEM((t, HEAD_DIM), F32),
            pltpu.VMEM((HEAD_DIM, 2 * HEAD_DIM), F32),
            pltpu.VMEM((HEAD_DIM, 2 * HEAD_DIM), F32),
            pltpu.VMEM((1, LANES), F32),
            pltpu.VMEM((1, LANES), F32),
        ],
        compiler_params=pltpu.CompilerParams(
            dimension_semantics=("parallel", "parallel"), vmem_limit_bytes=VMEM_LIMIT),
        name="mlstm",
    )(i_bias, f_bias, proj_big, proj_big, proj_big, proj_big, proj_big, gsm, conv_w, conv_w, norm_w)


def _out_proj_kernel(ya_ref, yb_ref, ga_ref, gb_ref, x_ref, wa_ref, wb_ref, wo_ref, gbias_ref, fw_ref,
                     o_ref):
    d = x_ref.shape[1]
    gbias = gbias_ref[...]
    a = jnp.dot(ya_ref[...], wa_ref[...], preferred_element_type=F32)
    b = jnp.dot(yb_ref[...], wb_ref[...], preferred_element_type=F32)
    merged = (jax.nn.sigmoid(ga_ref[...] + gbias[:, :d]) * a
              + jax.nn.sigmoid(gb_ref[...] + gbias[:, d:]) * b)
    xo = x_ref[...] + jnp.dot(merged.astype(BF16), wo_ref[...], preferred_element_type=F32)
    o_ref[...] = xo * lax.rsqrt(jnp.mean(xo * xo, axis=-1, keepdims=True) + NORM_EPS) * fw_ref[...]


def _out_proj(ya, yb, proj_big, x2, wa, wb, wo, gate_bias, final_w):
    m, d = x2.shape
    c = ya.shape[1]
    tm = min(256, m)
    const = lambda shape: pl.BlockSpec(shape, lambda i: (0, 0))
    return pl.pallas_call(
        _out_proj_kernel,
        grid=(m // tm,),
        in_specs=[
            pl.BlockSpec((tm, c), lambda i: (i, 0)),
            pl.BlockSpec((tm, c), lambda i: (i, 0)),
            pl.BlockSpec((tm, d), lambda i: (i, 0)),
            pl.BlockSpec((tm, d), lambda i: (i, 1)),
            pl.BlockSpec((tm, d), lambda i: (i, 0)),
            const((c, d)), const((c, d)), const((d, d)),
            const((1, 2 * d)), const((1, d)),
        ],
        out_specs=pl.BlockSpec((tm, d), lambda i: (i, 0)),
        out_shape=jax.ShapeDtypeStruct((m, d), F32),
        compiler_params=pltpu.CompilerParams(
            dimension_semantics=("parallel",), vmem_limit_bytes=VMEM_LIMIT),
        name="out_proj",
    )(ya, yb, proj_big, proj_big, x2, wa, wb, wo, gate_bias, final_w)


def kernel(x, w_in, conv_gdn, gdn_a_log, gdn_dt_bias, gdn_norm_w, conv_mlstm, mlstm_i_bias, mlstm_f_bias,
           mlstm_norm_w, gate_bias, w_branch_gdn, w_branch_mlstm, w_out, norm_w, final_norm_w):
    batch, t, d = x.shape
    depth = w_in.shape[0]
    assert depth == 1, "the final rmsnorm is fused into the single layer's output projection"
    key = HEADS * HEAD_DIM
    sizes = (3 * key, key, N_DIR * HEADS, N_DIR * HEADS, 2 * key, key, key, key, N_DIR * HEADS, N_DIR * HEADS,
             2 * d)
    offs = [0]
    for s in sizes:
        offs.append(offs[-1] + s)
    seg = lambda w, i: w[:, offs[i]:offs[i + 1]]

    x2 = x.reshape(batch * t, d)
    for layer in range(depth):
        w = w_in[layer]
        w_big = jnp.concatenate([seg(w, 10), seg(w, 0), seg(w, 1), seg(w, 4), seg(w, 5), seg(w, 6), seg(w, 7)],
                                axis=1).astype(BF16)
        w_small = jnp.concatenate([seg(w, 2), seg(w, 3), seg(w, 8), seg(w, 9)], axis=1)
        w_small = jnp.pad(w_small, ((0, 0), (0, LANES - w_small.shape[1])))
        proj_big, small = _in_proj(x2, norm_w[layer][None, :], w_big, w_small)
        gsm = small[:, :4 * N_DIR * HEADS].reshape(batch, t, 4, N_DIR, HEADS)
        gsm = gsm.transpose(0, 4, 1, 2, 3).reshape(batch, HEADS, t, 4 * N_DIR)
        ya = _gdn(proj_big, gsm, conv_gdn[layer], gdn_a_log[layer], gdn_dt_bias[layer],
                  gdn_norm_w[layer][None, :], batch, t)
        yb = _mlstm(proj_big, gsm, conv_mlstm[layer], mlstm_i_bias[layer], mlstm_f_bias[layer],
                    mlstm_norm_w[layer][None, :], batch, t)
        x2 = _out_proj(ya, yb, proj_big, x2, w_branch_gdn[layer].astype(BF16),
                       w_branch_mlstm[layer].astype(BF16), w_out[layer].astype(BF16),
                       gate_bias[layer][None, :], final_norm_w[None, :])
    return x2.reshape(batch, t, d)
```

```python
import functools

import jax
import jax.numpy as jnp
from jax import lax
from jax.experimental import pallas as pl
from jax.experimental.pallas import tpu as pltpu

F32 = jnp.float32
BF16 = jnp.bfloat16
HIGHEST = lax.Precision.HIGHEST

HEADS = 8
HEAD_DIM = 128
CHUNK = 64
CONV_WIDTH = 5
N_DIR = 2
NORM_EPS = 1e-6
LANES = 128
CONV_ROWS = 256
CONV_HALO = 8
CHUNK_GROUP = 4
VMEM_LIMIT = 56 * 1024 * 1024

COL_GATES = 0
COL_GQ, COL_GK, COL_GV, COL_GZ = 32, 40, 48, 56
COL_MQ, COL_MK, COL_MV, COL_MO, COL_MZ = 64, 72, 80, 88, 96
BIG_COLS = 104 * LANES


def _mm(a, b):
    return jnp.dot(a.astype(BF16), b.astype(BF16), preferred_element_type=F32)


def _mm_nt(a, b):
    return lax.dot_general(a.astype(BF16), b.astype(BF16), (((1,), (1,)), ((), ())),
                           preferred_element_type=F32)


def _mm_tn(a, b):
    return jnp.dot(a.T.astype(BF16), b.astype(BF16), preferred_element_type=F32)


def _silu(x):
    return x * jax.nn.sigmoid(x)


def _in_proj_kernel(x_ref, nw_ref, wbig_ref, wsmall_ref, big_ref, small_ref, n_scr):
    @pl.when(pl.program_id(1) == 0)
    def _():
        x = x_ref[...]
        y = x * lax.rsqrt(jnp.mean(x * x, axis=-1, keepdims=True) + NORM_EPS) * nw_ref[...]
        n_scr[...] = y.astype(BF16)
        small_ref[...] = jnp.dot(y, wsmall_ref[...], precision=HIGHEST, preferred_element_type=F32)

    big_ref[...] = jnp.dot(n_scr[...], wbig_ref[...], preferred_element_type=F32)


def _in_proj(x2, norm_w, w_big, w_small):
    m, d = x2.shape
    tm = min(1024, m)
    tn = 1024
    return pl.pallas_call(
        _in_proj_kernel,
        grid=(m // tm, BIG_COLS // tn),
        in_specs=[
            pl.BlockSpec((tm, d), lambda i, j: (i, 0)),
            pl.BlockSpec((1, d), lambda i, j: (0, 0)),
            pl.BlockSpec((d, tn), lambda i, j: (0, j)),
            pl.BlockSpec((d, LANES), lambda i, j: (0, 0)),
        ],
        out_specs=[
            pl.BlockSpec((tm, tn), lambda i, j: (i, j)),
            pl.BlockSpec((tm, LANES), lambda i, j: (i, 0)),
        ],
        out_shape=[
            jax.ShapeDtypeStruct((m, BIG_COLS), F32),
            jax.ShapeDtypeStruct((m, LANES), F32),
        ],
        scratch_shapes=[pltpu.VMEM((tm, d), BF16)],
        compiler_params=pltpu.CompilerParams(
            dimension_semantics=("parallel", "arbitrary"), vmem_limit_bytes=VMEM_LIMIT),
        name="in_proj",
    )(x2, norm_w, w_big, w_small)


def _conv_silu_into(src_ref, w_ref, pad_scr, dst_scr, post):
    t = src_ref.shape[0]
    rb = min(CONV_ROWS, t)
    win_rows = rb + 2 * CONV_HALO
    zeros = jnp.zeros((CONV_HALO, HEAD_DIM), F32)
    pad_scr[0:CONV_HALO, :] = zeros
    pad_scr[t + CONV_HALO:t + 2 * CONV_HALO, :] = zeros

    def copy(r, carry):
        off = pl.multiple_of(r * rb, rb)
        pad_scr[pl.ds(off + CONV_HALO, rb), :] = src_ref[pl.ds(off, rb), :]
        return carry

    lax.fori_loop(0, t // rb, copy, 0)
    w = w_ref[...]
    pad = (CONV_WIDTH - 1) // 2

    def body(r, carry):
        off = pl.multiple_of(r * rb, rb)
        win = pad_scr[pl.ds(off, win_rows), :]
        acc = None
        for j in range(CONV_WIDTH):
            shift = (pad - j) % win_rows
            rolled = pltpu.roll(win, shift, 0) if shift else win
            term = rolled[CONV_HALO:CONV_HALO + rb, :] * w[j:j + 1, :]
            acc = term if acc is None else acc + term
        dst_scr[pl.ds(off, rb), :] = post(_silu(acc))
        return carry

    lax.fori_loop(0, t // rb, body, 0)


def _direction_masks():
    row = lax.broadcasted_iota(jnp.int32, (CHUNK, CHUNK), 0)
    col = lax.broadcasted_iota(jnp.int32, (CHUNK, CHUNK), 1)
    fwd = (row >= col, row > col)
    bwd = (row <= col, row < col)
    out = []
    for (incl, strict), last in ((fwd, CHUNK - 1), (bwd, 0)):
        out.append((incl, strict, incl.astype(F32), last))
    return out


def _chunk_cumsum(tri, col):
    return jnp.dot(tri, jnp.broadcast_to(col, (CHUNK, LANES)), precision=HIGHEST,
                   preferred_element_type=F32)


def _as_rows(col_bcast):
    return col_bcast.T[:CHUNK, :]


def _each(fn, *columns):
    return [fn(*args) for args in zip(*columns)]


def _unit_triangular_inverses(lms, eye):
    inv = [eye - lm for lm in lms]
    power = lms
    exponent = 2
    while exponent < CHUNK:
        power = _each(lambda x: _mm(x, x), power)
        inv = _each(lambda p, x: p + _mm(p, x), inv, power)
        exponent *= 2
    return inv


def _gdn_kernel(alog_ref, dtb_ref, q_ref, k_ref, v_ref, z_ref, g_ref, cq_ref, ck_ref, cv_ref, nw_ref,
                y_ref, pad_scr, q_scr, k_scr, v_scr, o_scr):
    h = pl.program_id(1)
    t = q_ref.shape[0]
    n_chunks = t // CHUNK

    def l2(y):
        return y * lax.rsqrt(jnp.sum(y * y, axis=-1, keepdims=True) + NORM_EPS)

    _conv_silu_into(q_ref, cq_ref, pad_scr, q_scr, lambda y: l2(y) * (HEAD_DIM ** -0.5))
    _conv_silu_into(k_ref, ck_ref, pad_scr, k_scr, l2)
    _conv_silu_into(v_ref, cv_ref, pad_scr, v_scr, lambda y: y)

    o_scr[...] = jnp.zeros_like(o_scr)

    masks = _direction_masks()
    eye = (lax.broadcasted_iota(jnp.int32, (CHUNK, CHUNK), 0)
           == lax.broadcasted_iota(jnp.int32, (CHUNK, CHUNK), 1)).astype(F32)

    def chunk_terms(chains):
        dirs = [d for _, d in chains]
        offs = [pl.multiple_of(c * CHUNK, CHUNK) for c, _ in chains]
        qc = [q_scr[pl.ds(off, CHUNK), :] for off in offs]
        kc = [k_scr[pl.ds(off, CHUNK), :] for off in offs]
        vc = [v_scr[pl.ds(off, CHUNK), :] for off in offs]
        gt = [g_ref[pl.ds(off, CHUNK), :] for off in offs]
        beta = _each(lambda x, d: jax.nn.sigmoid(x[:, d:d + 1]), gt, dirs)

        def log_decay(x, d):
            a_log = jnp.full((CHUNK, 1), alog_ref[d, h], F32)
            dt_bias = jnp.full((CHUNK, 1), dtb_ref[d, h], F32)
            return -jnp.exp(a_log) * jax.nn.softplus(x[:, 2 + d:3 + d] + dt_bias)

        gcb = _each(lambda x, d: _chunk_cumsum(masks[d][2], log_decay(x, d)), gt, dirs)
        kq = _each(lambda k, q: _mm_nt(jnp.concatenate([k, q], axis=0), k), kc, qc)
        gc_rows = _each(_as_rows, gcb)
        decay = _each(lambda g, r, d: jnp.exp(jnp.where(masks[d][0], g[:, :CHUNK] - r, -jnp.inf)),
                      gcb, gc_rows, dirs)
        lm = _each(lambda x, b, dec, d: jnp.where(masks[d][1], x[:CHUNK] * b * dec, 0.0), kq, beta, decay, dirs)
        attn = _each(lambda x, dec: x[CHUNK:] * dec, kq, decay)
        egc = _each(jnp.exp, gcb)
        rhs = _each(lambda v, k, b, e: jnp.concatenate([v * b, k * b * e], axis=1), vc, kc, beta, egc)
        sol = _each(_mm, _unit_triangular_inverses(lm, eye), rhs)
        g_last = _each(lambda g, d: g[masks[d][3]:masks[d][3] + 1, :], gcb, dirs)
        k_dec_t = _each(lambda k, gl, g: (k * jnp.exp(gl - g)).T, kc, g_last, gcb)
        prod = _each(lambda a, kt, s: _mm(jnp.concatenate([a, kt], axis=0), s), attn, k_dec_t, sol)
        aq = _each(lambda p, q, e: jnp.concatenate(
            [p[CHUNK:, HEAD_DIM:], q * e - p[:CHUNK, HEAD_DIM:]], axis=0).astype(BF16), prod, qc, egc)
        return [(aq[i], prod[i][CHUNK:, :HEAD_DIM], jnp.exp(g_last[i]), prod[i][:CHUNK, :HEAD_DIM], offs[i])
                for i in range(len(chains))]

    def advance(state, terms):
        aq, b_mat, gamma, o_intra, off = terms
        r = jnp.dot(aq, state.astype(BF16), preferred_element_type=F32)
        o_scr[pl.ds(off, CHUNK), :] += o_intra + r[HEAD_DIM:]
        return state * gamma - r[:HEAD_DIM] + b_mat

    group = min(CHUNK_GROUP, n_chunks)

    def step(i, carry):
        s_fwd, s_bwd = carry
        chains = []
        for j in range(group):
            chains += [(i * group + j, 0), (n_chunks - 1 - (i * group + j), 1)]
        terms = chunk_terms(chains)
        for j in range(group):
            s_fwd = advance(s_fwd, terms[2 * j])
            s_bwd = advance(s_bwd, terms[2 * j + 1])
        return s_fwd, s_bwd

    zero_state = jnp.zeros((HEAD_DIM, HEAD_DIM), F32)
    lax.fori_loop(0, n_chunks // group, step, (zero_state, zero_state))

    rb = min(CONV_ROWS, t)
    nw = nw_ref[...]

    def epilogue(r, carry):
        off = pl.multiple_of(r * rb, rb)
        o = o_scr[pl.ds(off, rb), :]
        o = o * lax.rsqrt(jnp.mean(o * o, axis=-1, keepdims=True) + NORM_EPS) * nw
        y_ref[pl.ds(off, rb), :] = (o * _silu(z_ref[pl.ds(off, rb), :])).astype(y_ref.dtype)
        return carry

    lax.fori_loop(0, t // rb, epilogue, 0)


def _gdn(proj_big, gsm, conv_w, a_log, dt_bias, norm_w, batch, t):
    head_block = lambda base: pl.BlockSpec((t, HEAD_DIM), lambda b, h: (b, base + h))
    conv_block = lambda base: pl.BlockSpec((CONV_WIDTH, HEAD_DIM), lambda b, h: (0, base + h))
    smem = pl.BlockSpec(memory_space=pltpu.SMEM)
    return pl.pallas_call(
        _gdn_kernel,
        grid=(batch, HEADS),
        in_specs=[
            smem, smem,
            head_block(COL_GQ), head_block(COL_GK), head_block(COL_GV), head_block(COL_GZ),
            pl.BlockSpec((None, None, t, 4 * N_DIR), lambda b, h: (b, h, 0, 0)),
            conv_block(0), conv_block(HEADS), conv_block(2 * HEADS),
            pl.BlockSpec((1, HEAD_DIM), lambda b, h: (0, 0)),
        ],
        out_specs=pl.BlockSpec((t, HEAD_DIM), lambda b, h: (b, h)),
        out_shape=jax.ShapeDtypeStruct((batch * t, HEADS * HEAD_DIM), BF16),
        scratch_shapes=[
            pltpu.VMEM((t + 2 * CONV_HALO, HEAD_DIM), F32),
            pltpu.VMEM((t, HEAD_DIM), F32),
            pltpu.VMEM((t, HEAD_DIM), F32),
            pltpu.VMEM((t, HEAD_DIM), F32),
            pltpu.VMEM((t, HEAD_DIM), F32),
        ],
        compiler_params=pltpu.CompilerParams(
            dimension_semantics=("parallel", "parallel"), vmem_limit_bytes=VMEM_LIMIT),
        name="gdn",
    )(a_log, dt_bias, proj_big, proj_big, proj_big, proj_big, gsm, conv_w, conv_w, conv_w, norm_w)


def _mlstm_kernel(ib_ref, fb_ref, q_ref, k_ref, v_ref, op_ref, z_ref, g_ref, cq_ref, ck_ref, nw_ref,
                  y_ref, pad_scr, q_scr, k_scr, o_scr, cf_scr, cb_scr):
    h = pl.program_id(1)
    t = q_ref.shape[0]
    n_chunks = t // CHUNK

    _conv_silu_into(q_ref, cq_ref, pad_scr, q_scr, lambda y: y)
    _conv_silu_into(k_ref, ck_ref, pad_scr, k_scr, lambda y: y * (HEAD_DIM ** -0.5))

    o_scr[...] = jnp.zeros_like(o_scr)
    cf_scr[...] = jnp.zeros_like(cf_scr)
    cb_scr[...] = jnp.zeros_like(cb_scr)

    masks = _direction_masks()
    ones_col = (lax.broadcasted_iota(jnp.int32, (CHUNK, LANES), 1) == 0).astype(F32)

    def both(x):
        return jnp.concatenate([x, x], axis=1)

    def chunk_terms(chains):
        dirs = [d for _, d in chains]
        offs = [pl.multiple_of(c * CHUNK, CHUNK) for c, _ in chains]
        qc = [q_scr[pl.ds(off, CHUNK), :] for off in offs]
        kc = [k_scr[pl.ds(off, CHUNK), :] for off in offs]
        v_aug = [jnp.concatenate([v_ref[pl.ds(off, CHUNK), :], ones_col], axis=1) for off in offs]
        gt = [g_ref[pl.ds(off, CHUNK), :] for off in offs]

        def input_gate(x, d):
            i_bias = jnp.full((CHUNK, 1), ib_ref[d, h], F32)
            return jnp.broadcast_to(x[:, 4 + d:5 + d] + i_bias, (CHUNK, LANES))

        def log_forget(x, d):
            f_bias = jnp.full((CHUNK, 1), fb_ref[d, h], F32)
            return jax.nn.log_sigmoid(x[:, 6 + d:7 + d] + f_bias)

        igb = _each(input_gate, gt, dirs)
        bcb = _each(lambda x, d: _chunk_cumsum(masks[d][2], log_forget(x, d)), gt, dirs)
        qk = _each(_mm_nt, qc, kc)
        bc_rows = _each(_as_rows, bcb)
        ig_rows = _each(_as_rows, igb)
        d_log = _each(lambda b, br, ir, d: jnp.where(masks[d][0], b[:, :CHUNK] - br + ir, -jnp.inf),
                      bcb, bc_rows, ig_rows, dirs)
        m_intra = _each(lambda x: jnp.max(x, axis=1, keepdims=True), d_log)
        b_last = _each(lambda b, d: b[masks[d][3]:masks[d][3] + 1, :], bcb, dirs)
        a_end = _each(lambda bl, b, ig: bl - b + ig, b_last, bcb, igb)
        m_loc = _each(lambda x: jnp.max(x, axis=0, keepdims=True), a_end)
        k_end_t = _each(lambda k, a, m: (k * jnp.exp(a - m)).T, kc, a_end, m_loc)
        delta = _each(_mm, k_end_t, v_aug)
        intra = _each(lambda s, dl, m, v: _mm(s * jnp.exp(dl - m), v), qk, d_log, m_intra, v_aug)
        return [(qc[i].astype(BF16), bcb[i], m_intra[i], b_last[i], m_loc[i], delta[i], intra[i], offs[i])
                for i in range(len(chains))]

    def advance(state, terms):
        c_aug, m_st = state
        q_bf, bcb, m_intra, b_last, m_loc, delta, intra, off = terms
        m_inter = bcb + m_st
        m_t = jnp.maximum(m_inter, m_intra)
        w_inter = jnp.exp(m_inter - m_t)
        w_intra = jnp.exp(m_intra - m_t)
        inter = jnp.dot(q_bf, c_aug.astype(BF16), preferred_element_type=F32)
        num_aug = both(w_inter) * inter + both(w_intra) * intra
        num = num_aug[:, :HEAD_DIM]
        den = num_aug[:, HEAD_DIM:HEAD_DIM + 1]
        o_scr[pl.ds(off, CHUNK), :] += num / jnp.maximum(jnp.abs(den), jnp.exp(-m_t))
        m_new = jnp.maximum(b_last + m_st, m_loc)
        s_old = jnp.exp(b_last + m_st - m_new)
        s_loc = jnp.exp(m_loc - m_new)
        return c_aug * both(s_old) + delta * both(s_loc), m_new

    group = min(CHUNK_GROUP, n_chunks)

    def step(i, carry):
        m_fwd, m_bwd = carry
        chains = []
        for j in range(group):
            chains += [(i * group + j, 0), (n_chunks - 1 - (i * group + j), 1)]
        terms = chunk_terms(chains)
        s_fwd = (cf_scr[...], m_fwd)
        s_bwd = (cb_scr[...], m_bwd)
        for j in range(group):
            s_fwd = advance(s_fwd, terms[2 * j])
            s_bwd = advance(s_bwd, terms[2 * j + 1])
        cf_scr[...] = s_fwd[0]
        cb_scr[...] = s_bwd[0]
        return s_fwd[1], s_bwd[1]

    zero_m = jnp.zeros((1, LANES), F32)
    lax.fori_loop(0, n_chunks // group, step, (zero_m, zero_m))

    rb = min(CONV_ROWS, t)
    nw = nw_ref[...]

    def epilogue(r, carry):
        off = pl.multiple_of(r * rb, rb)
        hh = o_scr[pl.ds(off, rb), :]
        hc = hh - jnp.mean(hh, axis=-1, keepdims=True)
        hn = hc * lax.rsqrt(jnp.mean(hc * hc, axis=-1, keepdims=True) + NORM_EPS) * nw
        gate = jax.nn.sigmoid(op_ref[pl.ds(off, rb), :]) * _silu(z_ref[pl.ds(off, rb), :])
        y_ref[pl.ds(off, rb), :] = (hn * gate).astype(y_ref.dtype)
        return carry

    lax.fori_loop(0, t // rb, epilogue, 0)


def _mlstm(proj_big, gsm, conv_w, i_bias, f_bias, norm_w, batch, t):
    head_block = lambda base: pl.BlockSpec((t, HEAD_DIM), lambda b, h: (b, base + h))
    conv_block = lambda base: pl.BlockSpec((CONV_WIDTH, HEAD_DIM), lambda b, h: (0, base + h))
    smem = pl.BlockSpec(memory_space=pltpu.SMEM)
    return pl.pallas_call(
        _mlstm_kernel,
        grid=(batch, HEADS),
        in_specs=[
            smem, smem,
            head_block(COL_MQ), head_block(COL_MK), head_block(COL_MV), head_block(COL_MO),
            head_block(COL_MZ),
            pl.BlockSpec((None, None, t, 4 * N_DIR), lambda b, h: (b, h, 0, 0)),
            conv_block(0), conv_block(HEADS),
            pl.BlockSpec((1, HEAD_DIM), lambda b, h: (0, h)),
        ],
        out_specs=pl.BlockSpec((t, HEAD_DIM), lambda b, h: (b, h)),
        out_shape=jax.ShapeDtypeStruct((batch * t, HEADS * HEAD_DIM), BF16),
        scratch_shapes=[
            pltpu.VMEM((t + 2 * CONV_HALO, HEAD_DIM), F32),
            pltpu.VMEM((t, HEAD_DIM), F32),
            pltpu.VMEM((t, HEAD_DIM), F32),
            pltpu.VMEM((t, HEAD_DIM), F32),
            pltpu.VMEM((HEAD_DIM, 2 * HEAD_DIM), F32),
            pltpu.VMEM((HEAD_DIM, 2 * HEAD_DIM), F32),
        ],
        compiler_params=pltpu.CompilerParams(
            dimension_semantics=("parallel", "parallel"), vmem_limit_bytes=VMEM_LIMIT),
        name="mlstm",
    )(i_bias, f_bias, proj_big, proj_big, proj_big, proj_big, proj_big, gsm, conv_w, conv_w, norm_w)


def _out_proj_kernel(ya_ref, yb_ref, ga_ref, gb_ref, x_ref, wa_ref, wb_ref, wo_ref, gbias_ref, fw_ref,
                     o_ref):
    d = x_ref.shape[1]
    gbias = gbias_ref[...]
    a = jnp.dot(ya_ref[...], wa_ref[...], preferred_element_type=F32)
    b = jnp.dot(yb_ref[...], wb_ref[...], preferred_element_type=F32)
    merged = (jax.nn.sigmoid(ga_ref[...] + gbias[:, :d]) * a
              + jax.nn.sigmoid(gb_ref[...] + gbias[:, d:]) * b)
    xo = x_ref[...] + jnp.dot(merged.astype(BF16), wo_ref[...], preferred_element_type=F32)
    o_ref[...] = xo * lax.rsqrt(jnp.mean(xo * xo, axis=-1, keepdims=True) + NORM_EPS) * fw_ref[...]


def _out_proj(ya, yb, proj_big, x2, wa, wb, wo, gate_bias, final_w):
    m, d = x2.shape
    c = ya.shape[1]
    tm = min(256, m)
    const = lambda shape: pl.BlockSpec(shape, lambda i: (0, 0))
    return pl.pallas_call(
        _out_proj_kernel,
        grid=(m // tm,),
        in_specs=[
            pl.BlockSpec((tm, c), lambda i: (i, 0)),
            pl.BlockSpec((tm, c), lambda i: (i, 0)),
            pl.BlockSpec((tm, d), lambda i: (i, 0)),
            pl.BlockSpec((tm, d), lambda i: (i, 1)),
            pl.BlockSpec((tm, d), lambda i: (i, 0)),
            const((c, d)), const((c, d)), const((d, d)),
            const((1, 2 * d)), const((1, d)),
        ],
        out_specs=pl.BlockSpec((tm, d), lambda i: (i, 0)),
        out_shape=jax.ShapeDtypeStruct((m, d), F32),
        compiler_params=pltpu.CompilerParams(
            dimension_semantics=("parallel",), vmem_limit_bytes=VMEM_LIMIT),
        name="out_proj",
    )(ya, yb, proj_big, proj_big, x2, wa, wb, wo, gate_bias, final_w)


def kernel(x, w_in, conv_gdn, gdn_a_log, gdn_dt_bias, gdn_norm_w, conv_mlstm, mlstm_i_bias, mlstm_f_bias,
           mlstm_norm_w, gate_bias, w_branch_gdn, w_branch_mlstm, w_out, norm_w, final_norm_w):
    batch, t, d = x.shape
    depth = w_in.shape[0]
    assert depth == 1, "the final rmsnorm is fused into the single layer's output projection"
    key = HEADS * HEAD_DIM
    sizes = (3 * key, key, N_DIR * HEADS, N_DIR * HEADS, 2 * key, key, key, key, N_DIR * HEADS, N_DIR * HEADS,
             2 * d)
    offs = [0]
    for s in sizes:
        offs.append(offs[-1] + s)
    seg = lambda w, i: w[:, offs[i]:offs[i + 1]]

    x2 = x.reshape(batch * t, d)
    for layer in range(depth):
        w = w_in[layer]
        w_big = jnp.concatenate([seg(w, 10), seg(w, 0), seg(w, 1), seg(w, 4), seg(w, 5), seg(w, 6), seg(w, 7)],
                                axis=1).astype(BF16)
        w_small = jnp.concatenate([seg(w, 2), seg(w, 3), seg(w, 8), seg(w, 9)], axis=1)
        w_small = jnp.pad(w_small, ((0, 0), (0, LANES - w_small.shape[1])))
        proj_big, small = _in_proj(x2, norm_w[layer][None, :], w_big, w_small)
        gsm = small[:, :4 * N_DIR * HEADS].reshape(batch, t, 4, N_DIR, HEADS)
        gsm = gsm.transpose(0, 4, 1, 2, 3).reshape(batch, HEADS, t, 4 * N_DIR)
        ya = _gdn(proj_big, gsm, conv_gdn[layer], gdn_a_log[layer], gdn_dt_bias[layer],
                  gdn_norm_w[layer][None, :], batch, t)
        yb = _mlstm(proj_big, gsm, conv_mlstm[layer], mlstm_i_bias[layer], mlstm_f_bias[layer],
                    mlstm_norm_w[layer][None, :], batch, t)
        x2 = _out_proj(ya, yb, proj_big, x2, w_branch_gdn[layer].astype(BF16),
                       w_branch_mlstm[layer].astype(BF16), w_out[layer].astype(BF16),
                       gate_bias[layer][None, :], final_norm_w[None, :])
    return x2.reshape(batch, t, d)
```

```python
import functools

import jax
import jax.numpy as jnp
from jax import lax
from jax.experimental import pallas as pl
from jax.experimental.pallas import tpu as pltpu

F32 = jnp.float32
BF16 = jnp.bfloat16
HIGHEST = lax.Precision.HIGHEST

HEADS = 8
HEAD_DIM = 128
CHUNK = 64
CONV_WIDTH = 5
N_DIR = 2
NORM_EPS = 1e-6
LANES = 128
CONV_ROWS = 256
CONV_HALO = 8
CHUNK_GROUP = 8
AQ_ROWS = HEAD_DIM + CHUNK
VMEM_LIMIT = 56 * 1024 * 1024

COL_GATES = 0
COL_GQ, COL_GK, COL_GV, COL_GZ = 32, 40, 48, 56
COL_MQ, COL_MK, COL_MV, COL_MO, COL_MZ = 64, 72, 80, 88, 96
BIG_COLS = 104 * LANES


def _mm(a, b):
    return jnp.dot(a.astype(BF16), b.astype(BF16), preferred_element_type=F32)


def _mm_nt(a, b):
    return lax.dot_general(a.astype(BF16), b.astype(BF16), (((1,), (1,)), ((), ())),
                           preferred_element_type=F32)


def _mm_tn(a, b):
    return jnp.dot(a.T.astype(BF16), b.astype(BF16), preferred_element_type=F32)


def _silu(x):
    return x * jax.nn.sigmoid(x)


_NT_DIMS = (((1,), (1,)), ((), ()))


def _in_proj_kernel(x_ref, nw_ref, wbig_ref, wsmall_ref, big_ref, small_ref, n_scr):
    @pl.when(pl.program_id(1) == 0)
    def _():
        x = x_ref[...]
        y = x * lax.rsqrt(jnp.mean(x * x, axis=-1, keepdims=True) + NORM_EPS) * nw_ref[...]
        n_scr[...] = y.astype(BF16)
        small_ref[...] = lax.dot_general(y, wsmall_ref[...], _NT_DIMS, precision=HIGHEST,
                                         preferred_element_type=F32)

    big_ref[...] = lax.dot_general(n_scr[...], wbig_ref[...], _NT_DIMS, preferred_element_type=F32)


def _in_proj(x2, norm_w, wt_big, wt_small):
    m, d = x2.shape
    tm = min(1024, m)
    tn = 1024
    return pl.pallas_call(
        _in_proj_kernel,
        grid=(m // tm, BIG_COLS // tn),
        in_specs=[
            pl.BlockSpec((tm, d), lambda i, j: (i, 0)),
            pl.BlockSpec((1, d), lambda i, j: (0, 0)),
            pl.BlockSpec((tn, d), lambda i, j: (j, 0)),
            pl.BlockSpec((LANES, d), lambda i, j: (0, 0)),
        ],
        out_specs=[
            pl.BlockSpec((tm, tn), lambda i, j: (i, j)),
            pl.BlockSpec((tm, LANES), lambda i, j: (i, 0)),
        ],
        out_shape=[
            jax.ShapeDtypeStruct((m, BIG_COLS), F32),
            jax.ShapeDtypeStruct((m, LANES), F32),
        ],
        scratch_shapes=[pltpu.VMEM((tm, d), BF16)],
        compiler_params=pltpu.CompilerParams(
            dimension_semantics=("parallel", "arbitrary"), vmem_limit_bytes=VMEM_LIMIT),
        name="in_proj",
    )(x2, norm_w, wt_big, wt_small)


def _conv_silu_into(src_ref, w_ref, pad_scr, dst_scr, post):
    t = src_ref.shape[0]
    rb = min(CONV_ROWS, t)
    win_rows = rb + 2 * CONV_HALO
    zeros = jnp.zeros((CONV_HALO, HEAD_DIM), F32)
    pad_scr[0:CONV_HALO, :] = zeros
    pad_scr[t + CONV_HALO:t + 2 * CONV_HALO, :] = zeros

    def copy(r, carry):
        off = pl.multiple_of(r * rb, rb)
        pad_scr[pl.ds(off + CONV_HALO, rb), :] = src_ref[pl.ds(off, rb), :]
        return carry

    lax.fori_loop(0, t // rb, copy, 0)
    w = w_ref[...]
    pad = (CONV_WIDTH - 1) // 2

    def body(r, carry):
        off = pl.multiple_of(r * rb, rb)
        win = pad_scr[pl.ds(off, win_rows), :]
        acc = None
        for j in range(CONV_WIDTH):
            shift = (pad - j) % win_rows
            rolled = pltpu.roll(win, shift, 0) if shift else win
            term = rolled[CONV_HALO:CONV_HALO + rb, :] * w[j:j + 1, :]
            acc = term if acc is None else acc + term
        dst_scr[pl.ds(off, rb), :] = post(_silu(acc))
        return carry

    lax.fori_loop(0, t // rb, body, 0)


def _direction_masks():
    row = lax.broadcasted_iota(jnp.int32, (CHUNK, CHUNK), 0)
    col = lax.broadcasted_iota(jnp.int32, (CHUNK, CHUNK), 1)
    fwd = (row >= col, row > col)
    bwd = (row <= col, row < col)
    out = []
    for (incl, strict), last in ((fwd, CHUNK - 1), (bwd, 0)):
        tri = incl.astype(BF16)
        out.append((incl, strict, jnp.concatenate([tri, tri, tri], axis=1), last))
    return out


def _chunk_cumsum(tri3, col):
    x = jnp.broadcast_to(col, (CHUNK, LANES))
    hi = x.astype(BF16)
    rest = x - hi.astype(F32)
    mid = rest.astype(BF16)
    lo = (rest - mid.astype(F32)).astype(BF16)
    return jnp.dot(tri3, jnp.concatenate([hi, mid, lo], axis=0), preferred_element_type=F32)


def _as_rows(col_bcast):
    return col_bcast.T[:CHUNK, :]


def _each(fn, *columns):
    return [fn(*args) for args in zip(*columns)]


def _unit_triangular_inverses(lms, eye, between_levels):
    inv = [eye - lm for lm in lms]
    power = _each(lambda x: _mm(x, x), lms)
    between_levels()
    exponent = 2
    while 2 * exponent < CHUNK:
        both = _each(lambda p, x: _mm(jnp.concatenate([p, x], axis=0), x), inv, power)
        between_levels()
        inv = _each(lambda p, b: p + b[:CHUNK], inv, both)
        power = [b[CHUNK:] for b in both]
        exponent *= 2
    inv = _each(lambda p, x: p + _mm(p, x), inv, power)
    between_levels()
    return inv


def _gdn_kernel(alog_ref, dtb_ref, q_ref, k_ref, v_ref, z_ref, g_ref, cq_ref, ck_ref, cv_ref, nw_ref,
                y_ref, pad_scr, q_scr, k_scr, v_scr, o_scr, aq_pend, b_pend, o_pend, gam_pend):
    h = pl.program_id(1)
    t = q_ref.shape[0]
    n_chunks = t // CHUNK

    def l2(y):
        return y * lax.rsqrt(jnp.sum(y * y, axis=-1, keepdims=True) + NORM_EPS)

    _conv_silu_into(q_ref, cq_ref, pad_scr, q_scr, lambda y: l2(y) * (HEAD_DIM ** -0.5))
    _conv_silu_into(k_ref, ck_ref, pad_scr, k_scr, l2)
    _conv_silu_into(v_ref, cv_ref, pad_scr, v_scr, lambda y: y)

    o_scr[...] = jnp.zeros_like(o_scr)

    masks = _direction_masks()
    eye = (lax.broadcasted_iota(jnp.int32, (CHUNK, CHUNK), 0)
           == lax.broadcasted_iota(jnp.int32, (CHUNK, CHUNK), 1)).astype(F32)

    def chunk_terms(chains, between_stages):
        dirs = [d for _, d in chains]
        offs = [pl.multiple_of(c * CHUNK, CHUNK) for c, _ in chains]
        qc = [q_scr[pl.ds(off, CHUNK), :] for off in offs]
        kc = [k_scr[pl.ds(off, CHUNK), :] for off in offs]
        vc = [v_scr[pl.ds(off, CHUNK), :] for off in offs]
        gt = [g_ref[pl.ds(off, CHUNK), :] for off in offs]
        beta = _each(lambda x, d: jax.nn.sigmoid(x[:, d:d + 1]), gt, dirs)

        def log_decay(x, d):
            a_log = jnp.full((CHUNK, 1), alog_ref[d, h], F32)
            dt_bias = jnp.full((CHUNK, 1), dtb_ref[d, h], F32)
            return -jnp.exp(a_log) * jax.nn.softplus(x[:, 2 + d:3 + d] + dt_bias)

        gcb = _each(lambda x, d: _chunk_cumsum(masks[d][2], log_decay(x, d)), gt, dirs)
        between_stages()
        kq = _each(lambda k, q: _mm_nt(jnp.concatenate([k, q], axis=0), k), kc, qc)
        between_stages()
        gc_rows = _each(_as_rows, gcb)
        decay = _each(lambda g, r, d: jnp.exp(jnp.where(masks[d][0], g[:, :CHUNK] - r, -jnp.inf)),
                      gcb, gc_rows, dirs)
        lm = _each(lambda x, b, dec, d: jnp.where(masks[d][1], x[:CHUNK] * b * dec, 0.0), kq, beta, decay, dirs)
        attn = _each(lambda x, dec: x[CHUNK:] * dec, kq, decay)
        egc = _each(jnp.exp, gcb)
        rhs = _each(lambda v, k, b, e: jnp.concatenate([v * b, k * b * e], axis=1), vc, kc, beta, egc)
        sol = _each(_mm, _unit_triangular_inverses(lm, eye, between_stages), rhs)
        between_stages()
        g_last = _each(lambda g, d: g[masks[d][3]:masks[d][3] + 1, :], gcb, dirs)
        k_dec_t = _each(lambda k, gl, g: (k * jnp.exp(gl - g)).T, kc, g_last, gcb)
        prod = _each(lambda a, kt, s: _mm(jnp.concatenate([a, kt], axis=0), s), attn, k_dec_t, sol)
        between_stages()
        for slot in range(len(chains)):
            p = prod[slot]
            q_eff = qc[slot] * egc[slot] - p[:CHUNK, HEAD_DIM:]
            aq_pend[slot * AQ_ROWS:(slot + 1) * AQ_ROWS, :] = jnp.concatenate(
                [p[CHUNK:, HEAD_DIM:], q_eff], axis=0).astype(BF16)
            b_pend[slot * HEAD_DIM:(slot + 1) * HEAD_DIM, :] = p[CHUNK:, :HEAD_DIM]
            o_pend[slot * CHUNK:(slot + 1) * CHUNK, :] = p[:CHUNK, :HEAD_DIM]
            gam_pend[slot * 8:(slot + 1) * 8, :] = jnp.broadcast_to(jnp.exp(g_last[slot]), (8, LANES))

    def advance(state, slot, chunk):
        off = pl.multiple_of(chunk * CHUNK, CHUNK)
        r = jnp.dot(aq_pend[slot * AQ_ROWS:(slot + 1) * AQ_ROWS, :], state.astype(BF16),
                    preferred_element_type=F32)
        o_scr[pl.ds(off, CHUNK), :] += o_pend[slot * CHUNK:(slot + 1) * CHUNK, :] + r[HEAD_DIM:]
        return (state * gam_pend[slot * 8:slot * 8 + 1, :] - r[:HEAD_DIM]
                + b_pend[slot * HEAD_DIM:(slot + 1) * HEAD_DIM, :])

    group = min(CHUNK_GROUP, n_chunks)
    n_groups = n_chunks // group

    def chains_of(i):
        chains = []
        for j in range(group):
            chains += [(i * group + j, 0), (n_chunks - 1 - (i * group + j), 1)]
        return chains

    def state_walker(i, states):
        pending = list(range(group))

        def advance_one():
            if pending:
                j = pending.pop(0)
                states[0] = advance(states[0], 2 * j, i * group + j)
                states[1] = advance(states[1], 2 * j + 1, n_chunks - 1 - (i * group + j))

        return advance_one, pending

    chunk_terms(chains_of(0), lambda: None)

    def step(i, carry):
        states = list(carry)
        advance_one, pending = state_walker(i - 1, states)
        chunk_terms(chains_of(i), advance_one)
        assert not pending, "more chunks per group than matmul stages to hide them behind"
        return tuple(states)

    zero_state = jnp.zeros((HEAD_DIM, HEAD_DIM), F32)
    states = list(lax.fori_loop(1, n_groups, step, (zero_state, zero_state)))
    advance_one, pending = state_walker(n_groups - 1, states)
    while pending:
        advance_one()

    rb = min(CONV_ROWS, t)
    nw = nw_ref[...]

    def epilogue(r, carry):
        off = pl.multiple_of(r * rb, rb)
        o = o_scr[pl.ds(off, rb), :]
        o = o * lax.rsqrt(jnp.mean(o * o, axis=-1, keepdims=True) + NORM_EPS) * nw
        y_ref[pl.ds(off, rb), :] = (o * _silu(z_ref[pl.ds(off, rb), :])).astype(y_ref.dtype)
        return carry

    lax.fori_loop(0, t // rb, epilogue, 0)


def _gdn(proj_big, gsm, conv_w, a_log, dt_bias, norm_w, batch, t):
    slots = N_DIR * min(CHUNK_GROUP, t // CHUNK)
    head_block = lambda base: pl.BlockSpec((t, HEAD_DIM), lambda b, h: (b, base + h))
    conv_block = lambda base: pl.BlockSpec((CONV_WIDTH, HEAD_DIM), lambda b, h: (0, base + h))
    smem = pl.BlockSpec(memory_space=pltpu.SMEM)
    return pl.pallas_call(
        _gdn_kernel,
        grid=(batch, HEADS),
        in_specs=[
            smem, smem,
            head_block(COL_GQ), head_block(COL_GK), head_block(COL_GV), head_block(COL_GZ),
            pl.BlockSpec((None, None, t, 4 * N_DIR), lambda b, h: (b, h, 0, 0)),
            conv_block(0), conv_block(HEADS), conv_block(2 * HEADS),
            pl.BlockSpec((1, HEAD_DIM), lambda b, h: (0, 0)),
        ],
        out_specs=pl.BlockSpec((t, HEAD_DIM), lambda b, h: (b, h)),
        out_shape=jax.ShapeDtypeStruct((batch * t, HEADS * HEAD_DIM), BF16),
        scratch_shapes=[
            pltpu.VMEM((t + 2 * CONV_HALO, HEAD_DIM), F32),
            pltpu.VMEM((t, HEAD_DIM), F32),
            pltpu.VMEM((t, HEAD_DIM), F32),
            pltpu.VMEM((t, HEAD_DIM), F32),
            pltpu.VMEM((t, HEAD_DIM), F32),
            pltpu.VMEM((slots * AQ_ROWS, HEAD_DIM), BF16),
            pltpu.VMEM((slots * HEAD_DIM, HEAD_DIM), F32),
            pltpu.VMEM((slots * CHUNK, HEAD_DIM), F32),
            pltpu.VMEM((slots * 8, LANES), F32),
        ],
        compiler_params=pltpu.CompilerParams(
            dimension_semantics=("parallel", "parallel"), vmem_limit_bytes=VMEM_LIMIT),
        name="gdn",
    )(a_log, dt_bias, proj_big, proj_big, proj_big, proj_big, gsm, conv_w, conv_w, conv_w, norm_w)


def _mlstm_kernel(ib_ref, fb_ref, q_ref, k_ref, v_ref, op_ref, z_ref, g_ref, cq_ref, ck_ref, nw_ref,
                  y_ref, pad_scr, q_scr, k_scr, o_scr, cf_scr, cb_scr):
    h = pl.program_id(1)
    t = q_ref.shape[0]
    n_chunks = t // CHUNK

    _conv_silu_into(q_ref, cq_ref, pad_scr, q_scr, lambda y: y)
    _conv_silu_into(k_ref, ck_ref, pad_scr, k_scr, lambda y: y * (HEAD_DIM ** -0.5))

    o_scr[...] = jnp.zeros_like(o_scr)
    cf_scr[...] = jnp.zeros_like(cf_scr)
    cb_scr[...] = jnp.zeros_like(cb_scr)

    masks = _direction_masks()
    ones_col = (lax.broadcasted_iota(jnp.int32, (CHUNK, LANES), 1) == 0).astype(F32)

    def both(x):
        return jnp.concatenate([x, x], axis=1)

    def chunk_terms(chains):
        dirs = [d for _, d in chains]
        offs = [pl.multiple_of(c * CHUNK, CHUNK) for c, _ in chains]
        qc = [q_scr[pl.ds(off, CHUNK), :] for off in offs]
        kc = [k_scr[pl.ds(off, CHUNK), :] for off in offs]
        v_aug = [jnp.concatenate([v_ref[pl.ds(off, CHUNK), :], ones_col], axis=1) for off in offs]
        gt = [g_ref[pl.ds(off, CHUNK), :] for off in offs]

        def input_gate(x, d):
            i_bias = jnp.full((CHUNK, 1), ib_ref[d, h], F32)
            return jnp.broadcast_to(x[:, 4 + d:5 + d] + i_bias, (CHUNK, LANES))

        def log_forget(x, d):
            f_bias = jnp.full((CHUNK, 1), fb_ref[d, h], F32)
            return jax.nn.log_sigmoid(x[:, 6 + d:7 + d] + f_bias)

        igb = _each(input_gate, gt, dirs)
        bcb = _each(lambda x, d: _chunk_cumsum(masks[d][2], log_forget(x, d)), gt, dirs)
        qk = _each(_mm_nt, qc, kc)
        bc_rows = _each(_as_rows, bcb)
        ig_rows = _each(_as_rows, igb)
        d_log = _each(lambda b, br, ir, d: jnp.where(masks[d][0], b[:, :CHUNK] - br + ir, -jnp.inf),
                      bcb, bc_rows, ig_rows, dirs)
        m_intra = _each(lambda x: jnp.max(x, axis=1, keepdims=True), d_log)
        b_last = _each(lambda b, d: b[masks[d][3]:masks[d][3] + 1, :], bcb, dirs)
        a_end = _each(lambda bl, b, ig: bl - b + ig, b_last, bcb, igb)
        m_loc = _each(lambda x: jnp.max(x, axis=0, keepdims=True), a_end)
        k_end_t = _each(lambda k, a, m: (k * jnp.exp(a - m)).T, kc, a_end, m_loc)
        delta = _each(_mm, k_end_t, v_aug)
        intra = _each(lambda s, dl, m, v: _mm(s * jnp.exp(dl - m), v), qk, d_log, m_intra, v_aug)
        return [(qc[i].astype(BF16), bcb[i], m_intra[i], b_last[i], m_loc[i], delta[i], intra[i], offs[i])
                for i in range(len(chains))]

    def advance(state, terms):
        c_aug, m_st = state
        q_bf, bcb, m_intra, b_last, m_loc, delta, intra, off = terms
        m_inter = bcb + m_st
        m_t = jnp.maximum(m_inter, m_intra)
        w_inter = jnp.exp(m_inter - m_t)
        w_intra = jnp.exp(m_intra - m_t)
        inter = jnp.dot(q_bf, c_aug.astype(BF16), preferred_element_type=F32)
        num_aug = both(w_inter) * inter + both(w_intra) * intra
        num = num_aug[:, :HEAD_DIM]
        den = num_aug[:, HEAD_DIM:HEAD_DIM + 1]
        o_scr[pl.ds(off, CHUNK), :] += num / jnp.maximum(jnp.abs(den), jnp.exp(-m_t))
        m_new = jnp.maximum(b_last + m_st, m_loc)
        s_old = jnp.exp(b_last + m_st - m_new)
        s_loc = jnp.exp(m_loc - m_new)
        return c_aug * both(s_old) + delta * both(s_loc), m_new

    group = min(CHUNK_GROUP, n_chunks)

    def step(i, carry):
        m_fwd, m_bwd = carry
        chains = []
        for j in range(group):
            chains += [(i * group + j, 0), (n_chunks - 1 - (i * group + j), 1)]
        terms = chunk_terms(chains)
        s_fwd = (cf_scr[...], m_fwd)
        s_bwd = (cb_scr[...], m_bwd)
        for j in range(group):
            s_fwd = advance(s_fwd, terms[2 * j])
            s_bwd = advance(s_bwd, terms[2 * j + 1])
        cf_scr[...] = s_fwd[0]
        cb_scr[...] = s_bwd[0]
        return s_fwd[1], s_bwd[1]

    zero_m = jnp.zeros((1, LANES), F32)
    lax.fori_loop(0, n_chunks // group, step, (zero_m, zero_m))

    rb = min(CONV_ROWS, t)
    nw = nw_ref[...]

    def epilogue(r, carry):
        off = pl.multiple_of(r * rb, rb)
        hh = o_scr[pl.ds(off, rb), :]
        hc = hh - jnp.mean(hh, axis=-1, keepdims=True)
        hn = hc * lax.rsqrt(jnp.mean(hc * hc, axis=-1, keepdims=True) + NORM_EPS) * nw
        gate = jax.nn.sigmoid(op_ref[pl.ds(off, rb), :]) * _silu(z_ref[pl.ds(off, rb), :])
        y_ref[pl.ds(off, rb), :] = (hn * gate).astype(y_ref.dtype)
        return carry

    lax.fori_loop(0, t // rb, epilogue, 0)


def _mlstm(proj_big, gsm, conv_w, i_bias, f_bias, norm_w, batch, t):
    head_block = lambda base: pl.BlockSpec((t, HEAD_DIM), lambda b, h: (b, base + h))
    conv_block = lambda base: pl.BlockSpec((CONV_WIDTH, HEAD_DIM), lambda b, h: (0, base + h))
    smem = pl.BlockSpec(memory_space=pltpu.SMEM)
    return pl.pallas_call(
        _mlstm_kernel,
        grid=(batch, HEADS),
        in_specs=[
            smem, smem,
            head_block(COL_MQ), head_block(COL_MK), head_block(COL_MV), head_block(COL_MO),
            head_block(COL_MZ),
            pl.BlockSpec((None, None, t, 4 * N_DIR), lambda b, h: (b, h, 0, 0)),
            conv_block(0), conv_block(HEADS),
            pl.BlockSpec((1, HEAD_DIM), lambda b, h: (0, h)),
        ],
        out_specs=pl.BlockSpec((t, HEAD_DIM), lambda b, h: (b, h)),
        out_shape=jax.ShapeDtypeStruct((batch * t, HEADS * HEAD_DIM), BF16),
        scratch_shapes=[
            pltpu.VMEM((t + 2 * CONV_HALO, HEAD_DIM), F32),
            pltpu.VMEM((t, HEAD_DIM), F32),
            pltpu.VMEM((t, HEAD_DIM), F32),
            pltpu.VMEM((t, HEAD_DIM), F32),
            pltpu.VMEM((HEAD_DIM, 2 * HEAD_DIM), F32),
            pltpu.VMEM((HEAD_DIM, 2 * HEAD_DIM), F32),
        ],
        compiler_params=pltpu.CompilerParams(
            dimension_semantics=("parallel", "parallel"), vmem_limit_bytes=VMEM_LIMIT),
        name="mlstm",
    )(i_bias, f_bias, proj_big, proj_big, proj_big, proj_big, proj_big, gsm, conv_w, conv_w, norm_w)


def _out_proj_kernel(ya_ref, yb_ref, ga_ref, gb_ref, x_ref, wa_ref, wb_ref, wo_ref, gbias_ref, fw_ref,
                     o_ref):
    d = x_ref.shape[1]
    gbias = gbias_ref[...]
    a = jnp.dot(ya_ref[...], wa_ref[...], preferred_element_type=F32)
    b = jnp.dot(yb_ref[...], wb_ref[...], preferred_element_type=F32)
    merged = (jax.nn.sigmoid(ga_ref[...] + gbias[:, :d]) * a
              + jax.nn.sigmoid(gb_ref[...] + gbias[:, d:]) * b)
    xo = x_ref[...] + jnp.dot(merged.astype(BF16), wo_ref[...], preferred_element_type=F32)
    o_ref[...] = xo * lax.rsqrt(jnp.mean(xo * xo, axis=-1, keepdims=True) + NORM_EPS) * fw_ref[...]


def _out_proj(ya, yb, proj_big, x2, wa, wb, wo, gate_bias, final_w):
    m, d = x2.shape
    c = ya.shape[1]
    tm = min(256, m)
    const = lambda shape: pl.BlockSpec(shape, lambda i: (0, 0))
    return pl.pallas_call(
        _out_proj_kernel,
        grid=(m // tm,),
        in_specs=[
            pl.BlockSpec((tm, c), lambda i: (i, 0)),
            pl.BlockSpec((tm, c), lambda i: (i, 0)),
            pl.BlockSpec((tm, d), lambda i: (i, 0)),
            pl.BlockSpec((tm, d), lambda i: (i, 1)),
            pl.BlockSpec((tm, d), lambda i: (i, 0)),
            const((c, d)), const((c, d)), const((d, d)),
            const((1, 2 * d)), const((1, d)),
        ],
        out_specs=pl.BlockSpec((tm, d), lambda i: (i, 0)),
        out_shape=jax.ShapeDtypeStruct((m, d), F32),
        compiler_params=pltpu.CompilerParams(
            dimension_semantics=("parallel",), vmem_limit_bytes=VMEM_LIMIT),
        name="out_proj",
    )(ya, yb, proj_big, proj_big, x2, wa, wb, wo, gate_bias, final_w)


def kernel(x, w_in, conv_gdn, gdn_a_log, gdn_dt_bias, gdn_norm_w, conv_mlstm, mlstm_i_bias, mlstm_f_bias,
           mlstm_norm_w, gate_bias, w_branch_gdn, w_branch_mlstm, w_out, norm_w, final_norm_w):
    batch, t, d = x.shape
    depth = w_in.shape[0]
    assert depth == 1, "the final rmsnorm is fused into the single layer's output projection"
    key = HEADS * HEAD_DIM
    sizes = (3 * key, key, N_DIR * HEADS, N_DIR * HEADS, 2 * key, key, key, key, N_DIR * HEADS, N_DIR * HEADS,
             2 * d)
    offs = [0]
    for s in sizes:
        offs.append(offs[-1] + s)
    seg = lambda w, i: w[offs[i]:offs[i + 1], :]

    x2 = x.reshape(batch * t, d)
    for layer in range(depth):
        wt = jnp.swapaxes(w_in[layer], 0, 1)
        wt_big = jnp.concatenate([seg(wt, 10), seg(wt, 0), seg(wt, 1), seg(wt, 4), seg(wt, 5), seg(wt, 6),
                                  seg(wt, 7)], axis=0).astype(BF16)
        wt_small = jnp.concatenate([seg(wt, 2), seg(wt, 3), seg(wt, 8), seg(wt, 9)], axis=0)
        wt_small = jnp.pad(wt_small, ((0, LANES - wt_small.shape[0]), (0, 0)))
        proj_big, small = _in_proj(x2, norm_w[layer][None, :], wt_big, wt_small)
        gsm = small[:, :4 * N_DIR * HEADS].reshape(batch, t, 4, N_DIR, HEADS)
        gsm = gsm.transpose(0, 4, 1, 2, 3).reshape(batch, HEADS, t, 4 * N_DIR)
        ya = _gdn(proj_big, gsm, conv_gdn[layer], gdn_a_log[layer], gdn_dt_bias[layer],
                  gdn_norm_w[layer][None, :], batch, t)
        yb = _mlstm(proj_big, gsm, conv_mlstm[layer], mlstm_i_bias[layer], mlstm_f_bias[layer],
                    mlstm_norm_w[layer][None, :], batch, t)
        x2 = _out_proj(ya, yb, proj_big, x2, w_branch_gdn[layer].astype(BF16),
                       w_branch_mlstm[layer].astype(BF16), w_out[layer].astype(BF16),
                       gate_bias[layer][None, :], final_norm_w[None, :])
    return x2.reshape(batch, t, d)
```

```python
import functools

import jax
import jax.numpy as jnp
from jax import lax
from jax.experimental import pallas as pl
from jax.experimental.pallas import tpu as pltpu

F32 = jnp.float32
BF16 = jnp.bfloat16
HIGHEST = lax.Precision.HIGHEST

HEADS = 8
HEAD_DIM = 128
CHUNK = 64
CONV_WIDTH = 5
N_DIR = 2
NORM_EPS = 1e-6
LANES = 128
BF16_SUBLANES = 16
CONV_ROWS = 1024
CONV_HALO = 8
CHUNK_GROUP = 8
AQ_ROWS = HEAD_DIM + CHUNK
VMEM_LIMIT = 56 * 1024 * 1024

COL_GATES = 0
COL_GQ, COL_GK, COL_GV, COL_GZ = 32, 40, 48, 56
COL_MQ, COL_MK, COL_MV, COL_MO, COL_MZ = 64, 72, 80, 88, 96
BIG_COLS = 104 * LANES


def _mm(a, b):
    return jnp.dot(a.astype(BF16), b.astype(BF16), preferred_element_type=F32)


def _mm_nt(a, b):
    return lax.dot_general(a.astype(BF16), b.astype(BF16), (((1,), (1,)), ((), ())),
                           preferred_element_type=F32)


def _mm_tn(a, b):
    return jnp.dot(a.T.astype(BF16), b.astype(BF16), preferred_element_type=F32)


def _silu(x):
    return x * jax.nn.sigmoid(x)


_NT_DIMS = (((1,), (1,)), ((), ()))


def _in_proj_kernel(x_ref, nw_ref, wbig_ref, wsmall_ref, gpar_ref, big_ref, gates_ref, n_scr):
    @pl.when(pl.program_id(1) == 0)
    def _():
        x = x_ref[...]
        y = x * lax.rsqrt(jnp.mean(x * x, axis=-1, keepdims=True) + NORM_EPS) * nw_ref[...]
        n_scr[...] = y.astype(BF16)
        pre = lax.dot_general(y, wsmall_ref[...], _NT_DIMS, precision=HIGHEST, preferred_element_type=F32)
        z = pre + gpar_ref[1:2, :]
        col = lax.broadcasted_iota(jnp.int32, z.shape, 1)
        width = N_DIR * HEADS
        log_decay = -jnp.exp(gpar_ref[0:1, :]) * jax.nn.softplus(z)
        gates_ref[...] = jnp.where(col < width, jax.nn.sigmoid(z),
                                   jnp.where(col < 2 * width, log_decay,
                                             jnp.where(col < 3 * width, z, jax.nn.log_sigmoid(z))))

    big_ref[...] = lax.dot_general(n_scr[...], wbig_ref[...], _NT_DIMS, preferred_element_type=F32)


def _in_proj(x2, norm_w, wt, wt_small, gate_params, row_runs):
    m, d = x2.shape
    tm = min(1024, m)
    tn = 1024
    assert all(col % tn == 0 and row % BF16_SUBLANES == 0 for col, row in row_runs)

    def source_row(j):
        row = jnp.int32(0)
        for col, first_row in row_runs:
            row = jnp.where(j >= col // tn, first_row + (j - col // tn) * tn, row)
        return pl.multiple_of(row, BF16_SUBLANES)

    return pl.pallas_call(
        _in_proj_kernel,
        grid=(m // tm, BIG_COLS // tn),
        in_specs=[
            pl.BlockSpec((tm, d), lambda i, j: (i, 0)),
            pl.BlockSpec((1, d), lambda i, j: (0, 0)),
            pl.BlockSpec((pl.Element(tn), pl.Element(d)), lambda i, j: (source_row(j), 0)),
            pl.BlockSpec((LANES, d), lambda i, j: (0, 0)),
            pl.BlockSpec((2, LANES), lambda i, j: (0, 0)),
        ],
        out_specs=[
            pl.BlockSpec((tm, tn), lambda i, j: (i, j)),
            pl.BlockSpec((tm, LANES), lambda i, j: (i, 0)),
        ],
        out_shape=[
            jax.ShapeDtypeStruct((m, BIG_COLS), F32),
            jax.ShapeDtypeStruct((m, LANES), F32),
        ],
        scratch_shapes=[pltpu.VMEM((tm, d), BF16)],
        compiler_params=pltpu.CompilerParams(
            dimension_semantics=("parallel", "arbitrary"), vmem_limit_bytes=VMEM_LIMIT),
        name="in_proj",
    )(x2, norm_w, wt, wt_small, gate_params)


def _conv_silu_into(src_ref, w_ref, pad_scr, dst_scr, post):
    t = src_ref.shape[0]
    rb = min(CONV_ROWS, t)
    win_rows = rb + 2 * CONV_HALO
    zeros = jnp.zeros((CONV_HALO, HEAD_DIM), F32)
    pad_scr[0:CONV_HALO, :] = zeros
    pad_scr[t + CONV_HALO:t + 2 * CONV_HALO, :] = zeros

    def copy(r, carry):
        off = pl.multiple_of(r * rb, rb)
        pad_scr[pl.ds(off + CONV_HALO, rb), :] = src_ref[pl.ds(off, rb), :]
        return carry

    lax.fori_loop(0, t // rb, copy, 0)
    w = w_ref[...]
    pad = (CONV_WIDTH - 1) // 2

    def body(r, carry):
        off = pl.multiple_of(r * rb, rb)
        acc = None
        for j in range(CONV_WIDTH):
            term = pad_scr[pl.ds(off + (CONV_HALO - pad + j), rb), :] * w[j:j + 1, :]
            acc = term if acc is None else acc + term
        dst_scr[pl.ds(off, rb), :] = post(_silu(acc))
        return carry

    lax.fori_loop(0, t // rb, body, 0)


def _direction_masks():
    row = lax.broadcasted_iota(jnp.int32, (CHUNK, CHUNK), 0)
    col = lax.broadcasted_iota(jnp.int32, (CHUNK, CHUNK), 1)
    fwd = (row >= col, row > col)
    bwd = (row <= col, row < col)
    out = []
    for (incl, strict), last in ((fwd, CHUNK - 1), (bwd, 0)):
        tri = incl.astype(BF16)
        out.append((incl, strict, jnp.concatenate([tri, tri, tri], axis=1), last))
    return out


def _chunk_cumsum(tri3, col):
    x = jnp.broadcast_to(col, (CHUNK, LANES))
    hi = x.astype(BF16)
    rest = x - hi.astype(F32)
    mid = rest.astype(BF16)
    lo = (rest - mid.astype(F32)).astype(BF16)
    return jnp.dot(tri3, jnp.concatenate([hi, mid, lo], axis=0), preferred_element_type=F32)


def _as_rows(col_bcast):
    return col_bcast.T[:CHUNK, :]


def _each(fn, *columns):
    return [fn(*args) for args in zip(*columns)]


def _unit_triangular_inverses(lms, eye, between_levels):
    inv = [eye - lm for lm in lms]
    power = _each(lambda x: _mm(x, x), lms)
    between_levels()
    exponent = 2
    while 2 * exponent < CHUNK:
        both = _each(lambda p, x: _mm(jnp.concatenate([p, x], axis=0), x), inv, power)
        between_levels()
        inv = _each(lambda p, b: p + b[:CHUNK], inv, both)
        power = [b[CHUNK:] for b in both]
        exponent *= 2
    inv = _each(lambda p, x: p + _mm(p, x), inv, power)
    between_levels()
    return inv


def _gdn_kernel(q_ref, k_ref, v_ref, z_ref, g_ref, cq_ref, ck_ref, cv_ref, nw_ref,
                y_ref, pad_scr, q_scr, k_scr, v_scr, o_scr, aq_pend, b_pend, o_pend, gam_pend):
    t = q_ref.shape[0]
    n_chunks = t // CHUNK

    def l2(y):
        return y * lax.rsqrt(jnp.sum(y * y, axis=-1, keepdims=True) + NORM_EPS)

    _conv_silu_into(q_ref, cq_ref, pad_scr, q_scr, lambda y: l2(y) * (HEAD_DIM ** -0.5))
    _conv_silu_into(k_ref, ck_ref, pad_scr, k_scr, l2)
    _conv_silu_into(v_ref, cv_ref, pad_scr, v_scr, lambda y: y)

    o_scr[...] = jnp.zeros_like(o_scr)

    masks = _direction_masks()
    eye = (lax.broadcasted_iota(jnp.int32, (CHUNK, CHUNK), 0)
           == lax.broadcasted_iota(jnp.int32, (CHUNK, CHUNK), 1)).astype(F32)

    def chunk_terms(chains, between_stages):
        dirs = [d for _, d in chains]
        offs = [pl.multiple_of(c * CHUNK, CHUNK) for c, _ in chains]
        qc = [q_scr[pl.ds(off, CHUNK), :] for off in offs]
        kc = [k_scr[pl.ds(off, CHUNK), :] for off in offs]
        vc = [v_scr[pl.ds(off, CHUNK), :] for off in offs]
        gt = [g_ref[pl.ds(off, CHUNK), :] for off in offs]
        beta = _each(lambda x, d: x[:, d:d + 1], gt, dirs)
        gcb = _each(lambda x, d: _chunk_cumsum(masks[d][2], x[:, 2 + d:3 + d]), gt, dirs)
        between_stages()
        kq = _each(lambda k, q: _mm_nt(jnp.concatenate([k, q], axis=0), k), kc, qc)
        between_stages()
        gc_rows = _each(_as_rows, gcb)
        decay = _each(lambda g, r, d: jnp.exp(jnp.where(masks[d][0], g[:, :CHUNK] - r, -jnp.inf)),
                      gcb, gc_rows, dirs)
        lm = _each(lambda x, b, dec, d: jnp.where(masks[d][1], x[:CHUNK] * b * dec, 0.0), kq, beta, decay, dirs)
        attn = _each(lambda x, dec: x[CHUNK:] * dec, kq, decay)
        egc = _each(jnp.exp, gcb)
        rhs = _each(lambda v, k, b, e: jnp.concatenate([v * b, k * b * e], axis=1), vc, kc, beta, egc)
        sol = _each(_mm, _unit_triangular_inverses(lm, eye, between_stages), rhs)
        between_stages()
        g_last = _each(lambda g, d: g[masks[d][3]:masks[d][3] + 1, :], gcb, dirs)
        k_dec_t = _each(lambda k, gl, g: (k * jnp.exp(gl - g)).T, kc, g_last, gcb)
        prod = _each(lambda a, kt, s: _mm(jnp.concatenate([a, kt], axis=0), s), attn, k_dec_t, sol)
        between_stages()
        for slot in range(len(chains)):
            p = prod[slot]
            q_eff = qc[slot] * egc[slot] - p[:CHUNK, HEAD_DIM:]
            aq_pend[slot * AQ_ROWS:(slot + 1) * AQ_ROWS, :] = jnp.concatenate(
                [p[CHUNK:, HEAD_DIM:], q_eff], axis=0).astype(BF16)
            b_pend[slot * HEAD_DIM:(slot + 1) * HEAD_DIM, :] = p[CHUNK:, :HEAD_DIM]
            o_pend[slot * CHUNK:(slot + 1) * CHUNK, :] = p[:CHUNK, :HEAD_DIM]
            gam_pend[slot * 8:(slot + 1) * 8, :] = jnp.broadcast_to(jnp.exp(g_last[slot]), (8, LANES))

    def advance(state, slot, chunk):
        off = pl.multiple_of(chunk * CHUNK, CHUNK)
        r = jnp.dot(aq_pend[slot * AQ_ROWS:(slot + 1) * AQ_ROWS, :], state.astype(BF16),
                    preferred_element_type=F32)
        o_scr[pl.ds(off, CHUNK), :] += o_pend[slot * CHUNK:(slot + 1) * CHUNK, :] + r[HEAD_DIM:]
        return (state * gam_pend[slot * 8:slot * 8 + 1, :] - r[:HEAD_DIM]
                + b_pend[slot * HEAD_DIM:(slot + 1) * HEAD_DIM, :])

    group = min(CHUNK_GROUP, n_chunks)
    n_groups = n_chunks // group

    def chains_of(i):
        chains = []
        for j in range(group):
            chains += [(i * group + j, 0), (n_chunks - 1 - (i * group + j), 1)]
        return chains

    def state_walker(i, states):
        pending = list(range(group))

        def advance_one():
            if pending:
                j = pending.pop(0)
                states[0] = advance(states[0], 2 * j, i * group + j)
                states[1] = advance(states[1], 2 * j + 1, n_chunks - 1 - (i * group + j))

        return advance_one, pending

    chunk_terms(chains_of(0), lambda: None)

    def step(i, carry):
        states = list(carry)
        advance_one, pending = state_walker(i - 1, states)
        chunk_terms(chains_of(i), advance_one)
        assert not pending, "more chunks per group than matmul stages to hide them behind"
        return tuple(states)

    zero_state = jnp.zeros((HEAD_DIM, HEAD_DIM), F32)
    states = list(lax.fori_loop(1, n_groups, step, (zero_state, zero_state)))
    advance_one, pending = state_walker(n_groups - 1, states)
    while pending:
        advance_one()

    rb = min(CONV_ROWS, t)
    nw = nw_ref[...]

    def epilogue(r, carry):
        off = pl.multiple_of(r * rb, rb)
        o = o_scr[pl.ds(off, rb), :]
        o = o * lax.rsqrt(jnp.mean(o * o, axis=-1, keepdims=True) + NORM_EPS) * nw
        y_ref[pl.ds(off, rb), :] = (o * _silu(z_ref[pl.ds(off, rb), :])).astype(y_ref.dtype)
        return carry

    lax.fori_loop(0, t // rb, epilogue, 0)


def _gdn(proj_big, gsm, conv_w, norm_w, batch, t):
    slots = N_DIR * min(CHUNK_GROUP, t // CHUNK)
    head_block = lambda base: pl.BlockSpec((t, HEAD_DIM), lambda b, h: (b, base + h))
    conv_block = lambda base: pl.BlockSpec((CONV_WIDTH, HEAD_DIM), lambda b, h: (0, base + h))
    return pl.pallas_call(
        _gdn_kernel,
        grid=(batch, HEADS),
        in_specs=[
            head_block(COL_GQ), head_block(COL_GK), head_block(COL_GV), head_block(COL_GZ),
            pl.BlockSpec((None, None, t, 4 * N_DIR), lambda b, h: (b, h, 0, 0)),
            conv_block(0), conv_block(HEADS), conv_block(2 * HEADS),
            pl.BlockSpec((1, HEAD_DIM), lambda b, h: (0, 0)),
        ],
        out_specs=pl.BlockSpec((t, HEAD_DIM), lambda b, h: (b, h)),
        out_shape=jax.ShapeDtypeStruct((batch * t, HEADS * HEAD_DIM), BF16),
        scratch_shapes=[
            pltpu.VMEM((t + 2 * CONV_HALO, HEAD_DIM), F32),
            pltpu.VMEM((t, HEAD_DIM), F32),
            pltpu.VMEM((t, HEAD_DIM), F32),
            pltpu.VMEM((t, HEAD_DIM), F32),
            pltpu.VMEM((t, HEAD_DIM), F32),
            pltpu.VMEM((slots * AQ_ROWS, HEAD_DIM), BF16),
            pltpu.VMEM((slots * HEAD_DIM, HEAD_DIM), F32),
            pltpu.VMEM((slots * CHUNK, HEAD_DIM), F32),
            pltpu.VMEM((slots * 8, LANES), F32),
        ],
        compiler_params=pltpu.CompilerParams(
            dimension_semantics=("parallel", "parallel"), vmem_limit_bytes=VMEM_LIMIT),
        name="gdn",
    )(proj_big, proj_big, proj_big, proj_big, gsm, conv_w, conv_w, conv_w, norm_w)


def _mlstm_kernel(q_ref, k_ref, v_ref, op_ref, z_ref, g_ref, cq_ref, ck_ref, nw_ref,
                  y_ref, pad_scr, q_scr, k_scr, o_scr, cf_scr, cb_scr):
    t = q_ref.shape[0]
    n_chunks = t // CHUNK

    _conv_silu_into(q_ref, cq_ref, pad_scr, q_scr, lambda y: y)
    _conv_silu_into(k_ref, ck_ref, pad_scr, k_scr, lambda y: y * (HEAD_DIM ** -0.5))

    o_scr[...] = jnp.zeros_like(o_scr)
    cf_scr[...] = jnp.zeros_like(cf_scr)
    cb_scr[...] = jnp.zeros_like(cb_scr)

    masks = _direction_masks()
    ones_col = (lax.broadcasted_iota(jnp.int32, (CHUNK, LANES), 1) == 0).astype(F32)

    def both(x):
        return jnp.concatenate([x, x], axis=1)

    def chunk_terms(chains):
        dirs = [d for _, d in chains]
        offs = [pl.multiple_of(c * CHUNK, CHUNK) for c, _ in chains]
        qc = [q_scr[pl.ds(off, CHUNK), :] for off in offs]
        kc = [k_scr[pl.ds(off, CHUNK), :] for off in offs]
        v_aug = [jnp.concatenate([v_ref[pl.ds(off, CHUNK), :], ones_col], axis=1) for off in offs]
        gt = [g_ref[pl.ds(off, CHUNK), :] for off in offs]

        igb = _each(lambda x, d: jnp.broadcast_to(x[:, 4 + d:5 + d], (CHUNK, LANES)), gt, dirs)
        bcb = _each(lambda x, d: _chunk_cumsum(masks[d][2], x[:, 6 + d:7 + d]), gt, dirs)
        qk = _each(_mm_nt, qc, kc)
        bc_rows = _each(_as_rows, bcb)
        ig_rows = _each(_as_rows, igb)
        d_log = _each(lambda b, br, ir, d: jnp.where(masks[d][0], b[:, :CHUNK] - br + ir, -jnp.inf),
                      bcb, bc_rows, ig_rows, dirs)
        m_intra = _each(lambda x: jnp.max(x, axis=1, keepdims=True), d_log)
        b_last = _each(lambda b, d: b[masks[d][3]:masks[d][3] + 1, :], bcb, dirs)
        a_end = _each(lambda bl, b, ig: bl - b + ig, b_last, bcb, igb)
        m_loc = _each(lambda x: jnp.max(x, axis=0, keepdims=True), a_end)
        k_end_t = _each(lambda k, a, m: (k * jnp.exp(a - m)).T, kc, a_end, m_loc)
        delta = _each(_mm, k_end_t, v_aug)
        intra = _each(lambda s, dl, m, v: _mm(s * jnp.exp(dl - m), v), qk, d_log, m_intra, v_aug)
        return [(qc[i].astype(BF16), bcb[i], m_intra[i], b_last[i], m_loc[i], delta[i], intra[i], offs[i])
                for i in range(len(chains))]

    def advance(state, terms):
        c_aug, m_st = state
        q_bf, bcb, m_intra, b_last, m_loc, delta, intra, off = terms
        m_inter = bcb + m_st
        m_t = jnp.maximum(m_inter, m_intra)
        w_inter = jnp.exp(m_inter - m_t)
        w_intra = jnp.exp(m_intra - m_t)
        inter = jnp.dot(q_bf, c_aug.astype(BF16), preferred_element_type=F32)
        num_aug = both(w_inter) * inter + both(w_intra) * intra
        num = num_aug[:, :HEAD_DIM]
        den = num_aug[:, HEAD_DIM:HEAD_DIM + 1]
        o_scr[pl.ds(off, CHUNK), :] += num / jnp.maximum(jnp.abs(den), jnp.exp(-m_t))
        m_new = jnp.maximum(b_last + m_st, m_loc)
        s_old = jnp.exp(b_last + m_st - m_new)
        s_loc = jnp.exp(m_loc - m_new)
        return c_aug * both(s_old) + delta * both(s_loc), m_new

    group = min(CHUNK_GROUP, n_chunks)

    def step(i, carry):
        m_fwd, m_bwd = carry
        chains = []
        for j in range(group):
            chains += [(i * group + j, 0), (n_chunks - 1 - (i * group + j), 1)]
        terms = chunk_terms(chains)
        s_fwd = (cf_scr[...], m_fwd)
        s_bwd = (cb_scr[...], m_bwd)
        for j in range(group):
            s_fwd = advance(s_fwd, terms[2 * j])
            s_bwd = advance(s_bwd, terms[2 * j + 1])
        cf_scr[...] = s_fwd[0]
        cb_scr[...] = s_bwd[0]
        return s_fwd[1], s_bwd[1]

    zero_m = jnp.zeros((1, LANES), F32)
    lax.fori_loop(0, n_chunks // group, step, (zero_m, zero_m))

    rb = min(CONV_ROWS, t)
    nw = nw_ref[...]

    def epilogue(r, carry):
        off = pl.multiple_of(r * rb, rb)
        hh = o_scr[pl.ds(off, rb), :]
        hc = hh - jnp.mean(hh, axis=-1, keepdims=True)
        hn = hc * lax.rsqrt(jnp.mean(hc * hc, axis=-1, keepdims=True) + NORM_EPS) * nw
        gate = jax.nn.sigmoid(op_ref[pl.ds(off, rb), :]) * _silu(z_ref[pl.ds(off, rb), :])
        y_ref[pl.ds(off, rb), :] = (hn * gate).astype(y_ref.dtype)
        return carry

    lax.fori_loop(0, t // rb, epilogue, 0)


def _mlstm(proj_big, gsm, conv_w, norm_w, batch, t):
    head_block = lambda base: pl.BlockSpec((t, HEAD_DIM), lambda b, h: (b, base + h))
    conv_block = lambda base: pl.BlockSpec((CONV_WIDTH, HEAD_DIM), lambda b, h: (0, base + h))
    return pl.pallas_call(
        _mlstm_kernel,
        grid=(batch, HEADS),
        in_specs=[
            head_block(COL_MQ), head_block(COL_MK), head_block(COL_MV), head_block(COL_MO),
            head_block(COL_MZ),
            pl.BlockSpec((None, None, t, 4 * N_DIR), lambda b, h: (b, h, 0, 0)),
            conv_block(0), conv_block(HEADS),
            pl.BlockSpec((1, HEAD_DIM), lambda b, h: (0, h)),
        ],
        out_specs=pl.BlockSpec((t, HEAD_DIM), lambda b, h: (b, h)),
        out_shape=jax.ShapeDtypeStruct((batch * t, HEADS * HEAD_DIM), BF16),
        scratch_shapes=[
            pltpu.VMEM((t + 2 * CONV_HALO, HEAD_DIM), F32),
            pltpu.VMEM((t, HEAD_DIM), F32),
            pltpu.VMEM((t, HEAD_DIM), F32),
            pltpu.VMEM((t, HEAD_DIM), F32),
            pltpu.VMEM((HEAD_DIM, 2 * HEAD_DIM), F32),
            pltpu.VMEM((HEAD_DIM, 2 * HEAD_DIM), F32),
        ],
        compiler_params=pltpu.CompilerParams(
            dimension_semantics=("parallel", "parallel"), vmem_limit_bytes=VMEM_LIMIT),
        name="mlstm",
    )(proj_big, proj_big, proj_big, proj_big, proj_big, gsm, conv_w, conv_w, norm_w)


def _out_proj_kernel(ya_ref, yb_ref, ga_ref, gb_ref, x_ref, wa_ref, wb_ref, wo_ref, gbias_ref, fw_ref,
                     o_ref):
    d = x_ref.shape[1]
    gbias = gbias_ref[...]
    a = jnp.dot(ya_ref[...], wa_ref[...], preferred_element_type=F32)
    b = jnp.dot(yb_ref[...], wb_ref[...], preferred_element_type=F32)
    merged = (jax.nn.sigmoid(ga_ref[...] + gbias[:, :d]) * a
              + jax.nn.sigmoid(gb_ref[...] + gbias[:, d:]) * b)
    xo = x_ref[...] + jnp.dot(merged.astype(BF16), wo_ref[...], preferred_element_type=F32)
    o_ref[...] = xo * lax.rsqrt(jnp.mean(xo * xo, axis=-1, keepdims=True) + NORM_EPS) * fw_ref[...]


def _out_proj(ya, yb, proj_big, x2, wa, wb, wo, gate_bias, final_w):
    m, d = x2.shape
    c = ya.shape[1]
    tm = min(256, m)
    const = lambda shape: pl.BlockSpec(shape, lambda i: (0, 0))
    return pl.pallas_call(
        _out_proj_kernel,
        grid=(m // tm,),
        in_specs=[
            pl.BlockSpec((tm, c), lambda i: (i, 0)),
            pl.BlockSpec((tm, c), lambda i: (i, 0)),
            pl.BlockSpec((tm, d), lambda i: (i, 0)),
            pl.BlockSpec((tm, d), lambda i: (i, 1)),
            pl.BlockSpec((tm, d), lambda i: (i, 0)),
            const((c, d)), const((c, d)), const((d, d)),
            const((1, 2 * d)), const((1, d)),
        ],
        out_specs=pl.BlockSpec((tm, d), lambda i: (i, 0)),
        out_shape=jax.ShapeDtypeStruct((m, d), F32),
        compiler_params=pltpu.CompilerParams(
            dimension_semantics=("parallel",), vmem_limit_bytes=VMEM_LIMIT),
        name="out_proj",
    )(ya, yb, proj_big, proj_big, x2, wa, wb, wo, gate_bias, final_w)


def kernel(x, w_in, conv_gdn, gdn_a_log, gdn_dt_bias, gdn_norm_w, conv_mlstm, mlstm_i_bias, mlstm_f_bias,
           mlstm_norm_w, gate_bias, w_branch_gdn, w_branch_mlstm, w_out, norm_w, final_norm_w):
    batch, t, d = x.shape
    depth = w_in.shape[0]
    assert depth == 1, "the final rmsnorm is fused into the single layer's output projection"
    key = HEADS * HEAD_DIM
    sizes = (3 * key, key, N_DIR * HEADS, N_DIR * HEADS, 2 * key, key, key, key, N_DIR * HEADS, N_DIR * HEADS,
             2 * d)
    offs = [0]
    for s in sizes:
        offs.append(offs[-1] + s)
    seg = lambda w, i: w[offs[i]:offs[i + 1], :]

    x2 = x.reshape(batch * t, d)
    for layer in range(depth):
        wt = jnp.swapaxes(w_in[layer], 0, 1)
        wt_small = jnp.concatenate([seg(wt, 2), seg(wt, 3), seg(wt, 8), seg(wt, 9)], axis=0)
        wt_small = jnp.pad(wt_small, ((0, LANES - wt_small.shape[0]), (0, 0)))
        row_runs = ((0, offs[10]), (2 * d, offs[0]), (2 * d + 4 * key, offs[4]))
        width = N_DIR * HEADS
        zeros = jnp.zeros((width,), F32)
        gate_params = jnp.stack([
            jnp.concatenate([zeros, gdn_a_log[layer].reshape(-1), zeros, zeros]),
            jnp.concatenate([zeros, gdn_dt_bias[layer].reshape(-1), mlstm_i_bias[layer].reshape(-1),
                             mlstm_f_bias[layer].reshape(-1)])])
        gate_params = jnp.pad(gate_params, ((0, 0), (0, LANES - 4 * width)))
        proj_big, small = _in_proj(x2, norm_w[layer][None, :], wt.astype(BF16), wt_small, gate_params, row_runs)
        gsm = small[:, :4 * N_DIR * HEADS].reshape(batch, t, 4, N_DIR, HEADS)
        gsm = gsm.transpose(0, 4, 1, 2, 3).reshape(batch, HEADS, t, 4 * N_DIR)
        ya = _gdn(proj_big, gsm, conv_gdn[layer], gdn_norm_w[layer][None, :], batch, t)
        yb = _mlstm(proj_big, gsm, conv_mlstm[layer], mlstm_norm_w[layer][None, :], batch, t)
        x2 = _out_proj(ya, yb, proj_big, x2, w_branch_gdn[layer].astype(BF16),
                       w_branch_mlstm[layer].astype(BF16), w_out[layer].astype(BF16),
                       gate_bias[layer][None, :], final_norm_w[None, :])
    return x2.reshape(batch, t, d)
```

```python
import functools

import jax
import jax.numpy as jnp
from jax import lax
from jax.experimental import pallas as pl
from jax.experimental.pallas import tpu as pltpu

F32 = jnp.float32
BF16 = jnp.bfloat16

HEADS = 8
HEAD_DIM = 128
CHUNK = 64
CONV_WIDTH = 5
N_DIR = 2
NORM_EPS = 1e-6
LANES = 128
BF16_SUBLANES = 16
CONV_ROWS = 1024
CONV_HALO = 8
CHUNK_GROUP = 8
AQ_ROWS = HEAD_DIM + CHUNK
VMEM_LIMIT = 56 * 1024 * 1024

COL_GATES = 0
COL_GQ, COL_GK, COL_GV, COL_GZ = 32, 40, 48, 56
COL_MQ, COL_MK, COL_MV, COL_MO, COL_MZ = 64, 72, 80, 88, 96
BIG_COLS = 104 * LANES


def _mm(a, b):
    return jnp.dot(a.astype(BF16), b.astype(BF16), preferred_element_type=F32)


def _mm_nt(a, b):
    return lax.dot_general(a.astype(BF16), b.astype(BF16), (((1,), (1,)), ((), ())),
                           preferred_element_type=F32)


def _mm_tn(a, b):
    return jnp.dot(a.T.astype(BF16), b.astype(BF16), preferred_element_type=F32)


def _silu(x):
    return x * jax.nn.sigmoid(x)


_NT_DIMS = (((1,), (1,)), ((), ()))


def _in_proj_kernel(x_ref, nw_ref, wbig_ref, wsmall_ref, gpar_ref, big_ref, gates_ref, n_scr):
    @pl.when(pl.program_id(1) == 0)
    def _():
        x = x_ref[...]
        y = x * lax.rsqrt(jnp.mean(x * x, axis=-1, keepdims=True) + NORM_EPS) * nw_ref[...]
        n_scr[...] = y.astype(BF16)
        pre = lax.dot_general(n_scr[...], wsmall_ref[...], _NT_DIMS, preferred_element_type=F32)
        z = pre + gpar_ref[1:2, :]
        col = lax.broadcasted_iota(jnp.int32, z.shape, 1)
        width = N_DIR * HEADS
        log_decay = -jnp.exp(gpar_ref[0:1, :]) * jax.nn.softplus(z)
        gates_ref[...] = jnp.where(col < width, jax.nn.sigmoid(z),
                                   jnp.where(col < 2 * width, log_decay,
                                             jnp.where(col < 3 * width, z, jax.nn.log_sigmoid(z))))

    big_ref[...] = lax.dot_general(n_scr[...], wbig_ref[...], _NT_DIMS, preferred_element_type=F32)


def _in_proj(x2, norm_w, wt, wt_small, gate_params, row_runs):
    m, d = x2.shape
    tm = min(1024, m)
    tn = 1024
    assert all(col % tn == 0 and row % BF16_SUBLANES == 0 for col, row in row_runs)

    def source_row(j):
        row = jnp.int32(0)
        for col, first_row in row_runs:
            row = jnp.where(j >= col // tn, first_row + (j - col // tn) * tn, row)
        return pl.multiple_of(row, BF16_SUBLANES)

    return pl.pallas_call(
        _in_proj_kernel,
        grid=(m // tm, BIG_COLS // tn),
        in_specs=[
            pl.BlockSpec((tm, d), lambda i, j: (i, 0)),
            pl.BlockSpec((1, d), lambda i, j: (0, 0)),
            pl.BlockSpec((pl.Element(tn), pl.Element(d)), lambda i, j: (source_row(j), 0)),
            pl.BlockSpec((LANES, d), lambda i, j: (0, 0)),
            pl.BlockSpec((2, LANES), lambda i, j: (0, 0)),
        ],
        out_specs=[
            pl.BlockSpec((tm, tn), lambda i, j: (i, j)),
            pl.BlockSpec((tm, LANES), lambda i, j: (i, 0)),
        ],
        out_shape=[
            jax.ShapeDtypeStruct((m, BIG_COLS), F32),
            jax.ShapeDtypeStruct((m, LANES), F32),
        ],
        scratch_shapes=[pltpu.VMEM((tm, d), BF16)],
        compiler_params=pltpu.CompilerParams(
            dimension_semantics=("parallel", "arbitrary"), vmem_limit_bytes=VMEM_LIMIT),
        name="in_proj",
    )(x2, norm_w, wt, wt_small, gate_params)


def _conv_silu_into(src_ref, w_ref, pad_scr, dst_scr, post):
    t = src_ref.shape[0]
    rb = min(CONV_ROWS, t)
    win_rows = rb + 2 * CONV_HALO
    zeros = jnp.zeros((CONV_HALO, HEAD_DIM), F32)
    pad_scr[0:CONV_HALO, :] = zeros
    pad_scr[t + CONV_HALO:t + 2 * CONV_HALO, :] = zeros

    def copy(r, carry):
        off = pl.multiple_of(r * rb, rb)
        pad_scr[pl.ds(off + CONV_HALO, rb), :] = src_ref[pl.ds(off, rb), :]
        return carry

    lax.fori_loop(0, t // rb, copy, 0)
    w = w_ref[...]
    pad = (CONV_WIDTH - 1) // 2

    def body(r, carry):
        off = pl.multiple_of(r * rb, rb)
        acc = None
        for j in range(CONV_WIDTH):
            term = pad_scr[pl.ds(off + (CONV_HALO - pad + j), rb), :] * w[j:j + 1, :]
            acc = term if acc is None else acc + term
        dst_scr[pl.ds(off, rb), :] = post(_silu(acc))
        return carry

    lax.fori_loop(0, t // rb, body, 0)


def _direction_masks():
    row = lax.broadcasted_iota(jnp.int32, (CHUNK, CHUNK), 0)
    col = lax.broadcasted_iota(jnp.int32, (CHUNK, CHUNK), 1)
    fwd = (row >= col, row > col)
    bwd = (row <= col, row < col)
    out = []
    for (incl, strict), last in ((fwd, CHUNK - 1), (bwd, 0)):
        tri = incl.astype(BF16)
        out.append((incl, strict, jnp.concatenate([tri, tri, tri], axis=1), last))
    return out


def _chunk_cumsum(tri3, col):
    x = jnp.broadcast_to(col, (CHUNK, LANES))
    hi = x.astype(BF16)
    rest = x - hi.astype(F32)
    mid = rest.astype(BF16)
    lo = (rest - mid.astype(F32)).astype(BF16)
    return jnp.dot(tri3, jnp.concatenate([hi, mid, lo], axis=0), preferred_element_type=F32)


def _as_rows(col_bcast):
    return col_bcast.T[:CHUNK, :]


def _each(fn, *columns):
    return [fn(*args) for args in zip(*columns)]


def _unit_triangular_inverses(lms, eye, between_levels):
    inv = [eye - lm for lm in lms]
    power = _each(lambda x: _mm(x, x), lms)
    between_levels()
    exponent = 2
    while 2 * exponent < CHUNK:
        both = _each(lambda p, x: _mm(jnp.concatenate([p, x], axis=0), x), inv, power)
        between_levels()
        inv = _each(lambda p, b: p + b[:CHUNK], inv, both)
        power = [b[CHUNK:] for b in both]
        exponent *= 2
    inv = _each(lambda p, x: p + _mm(p, x), inv, power)
    between_levels()
    return inv


def _gdn_kernel(q_ref, k_ref, v_ref, z_ref, g_ref, cq_ref, ck_ref, cv_ref, nw_ref,
                y_ref, pad_scr, q_scr, k_scr, v_scr, o_scr, aq_pend, b_pend, o_pend, gam_pend):
    t = q_ref.shape[0]
    n_chunks = t // CHUNK

    def l2(y):
        return y * lax.rsqrt(jnp.sum(y * y, axis=-1, keepdims=True) + NORM_EPS)

    _conv_silu_into(q_ref, cq_ref, pad_scr, q_scr, lambda y: l2(y) * (HEAD_DIM ** -0.5))
    _conv_silu_into(k_ref, ck_ref, pad_scr, k_scr, l2)
    _conv_silu_into(v_ref, cv_ref, pad_scr, v_scr, lambda y: y)

    o_scr[...] = jnp.zeros_like(o_scr)

    masks = _direction_masks()
    eye = (lax.broadcasted_iota(jnp.int32, (CHUNK, CHUNK), 0)
           == lax.broadcasted_iota(jnp.int32, (CHUNK, CHUNK), 1)).astype(F32)

    def chunk_terms(chains, between_stages):
        dirs = [d for _, d in chains]
        offs = [pl.multiple_of(c * CHUNK, CHUNK) for c, _ in chains]
        qc = [q_scr[pl.ds(off, CHUNK), :] for off in offs]
        kc = [k_scr[pl.ds(off, CHUNK), :] for off in offs]
        vc = [v_scr[pl.ds(off, CHUNK), :] for off in offs]
        gt = [g_ref[pl.ds(off, CHUNK), :] for off in offs]
        beta = _each(lambda x, d: x[:, d:d + 1], gt, dirs)
        gcb = _each(lambda x, d: _chunk_cumsum(masks[d][2], x[:, 2 + d:3 + d]), gt, dirs)
        between_stages()
        kq = _each(lambda k, q: _mm_nt(jnp.concatenate([k, q], axis=0), k), kc, qc)
        between_stages()
        gc_rows = _each(_as_rows, gcb)
        decay = _each(lambda g, r, d: jnp.exp(jnp.where(masks[d][0], g[:, :CHUNK] - r, -jnp.inf)),
                      gcb, gc_rows, dirs)
        lm = _each(lambda x, b, dec, d: jnp.where(masks[d][1], x[:CHUNK] * b * dec, 0.0), kq, beta, decay, dirs)
        attn = _each(lambda x, dec: x[CHUNK:] * dec, kq, decay)
        egc = _each(jnp.exp, gcb)
        rhs = _each(lambda v, k, b, e: jnp.concatenate([v * b, k * b * e], axis=1), vc, kc, beta, egc)
        sol = _each(_mm, _unit_triangular_inverses(lm, eye, between_stages), rhs)
        between_stages()
        g_last = _each(lambda g, d: g[masks[d][3]:masks[d][3] + 1, :], gcb, dirs)
        k_dec_t = _each(lambda k, gl, g: (k * jnp.exp(gl - g)).T, kc, g_last, gcb)
        prod = _each(lambda a, kt, s: _mm(jnp.concatenate([a, kt], axis=0), s), attn, k_dec_t, sol)
        between_stages()
        for slot in range(len(chains)):
            p = prod[slot]
            q_eff = qc[slot] * egc[slot] - p[:CHUNK, HEAD_DIM:]
            aq_pend[slot * AQ_ROWS:(slot + 1) * AQ_ROWS, :] = jnp.concatenate(
                [p[CHUNK:, HEAD_DIM:], q_eff], axis=0).astype(BF16)
            b_pend[slot * HEAD_DIM:(slot + 1) * HEAD_DIM, :] = p[CHUNK:, :HEAD_DIM]
            o_pend[slot * CHUNK:(slot + 1) * CHUNK, :] = p[:CHUNK, :HEAD_DIM]
            gam_pend[slot * 8:(slot + 1) * 8, :] = jnp.broadcast_to(jnp.exp(g_last[slot]), (8, LANES))

    def advance(state, slot, chunk):
        off = pl.multiple_of(chunk * CHUNK, CHUNK)
        r = jnp.dot(aq_pend[slot * AQ_ROWS:(slot + 1) * AQ_ROWS, :], state.astype(BF16),
                    preferred_element_type=F32)
        o_scr[pl.ds(off, CHUNK), :] += o_pend[slot * CHUNK:(slot + 1) * CHUNK, :] + r[HEAD_DIM:]
        return (state * gam_pend[slot * 8:slot * 8 + 1, :] - r[:HEAD_DIM]
                + b_pend[slot * HEAD_DIM:(slot + 1) * HEAD_DIM, :])

    group = min(CHUNK_GROUP, n_chunks)
    n_groups = n_chunks // group

    def chains_of(i):
        chains = []
        for j in range(group):
            chains += [(i * group + j, 0), (n_chunks - 1 - (i * group + j), 1)]
        return chains

    def state_walker(i, states):
        pending = list(range(group))

        def advance_one():
            if pending:
                j = pending.pop(0)
                states[0] = advance(states[0], 2 * j, i * group + j)
                states[1] = advance(states[1], 2 * j + 1, n_chunks - 1 - (i * group + j))

        return advance_one, pending

    chunk_terms(chains_of(0), lambda: None)

    def step(i, carry):
        states = list(carry)
        advance_one, pending = state_walker(i - 1, states)
        chunk_terms(chains_of(i), advance_one)
        assert not pending, "more chunks per group than matmul stages to hide them behind"
        return tuple(states)

    zero_state = jnp.zeros((HEAD_DIM, HEAD_DIM), F32)
    states = list(lax.fori_loop(1, n_groups, step, (zero_state, zero_state)))
    advance_one, pending = state_walker(n_groups - 1, states)
    while pending:
        advance_one()

    rb = min(CONV_ROWS, t)
    nw = nw_ref[...]

    def epilogue(r, carry):
        off = pl.multiple_of(r * rb, rb)
        o = o_scr[pl.ds(off, rb), :]
        o = o * lax.rsqrt(jnp.mean(o * o, axis=-1, keepdims=True) + NORM_EPS) * nw
        y_ref[pl.ds(off, rb), :] = (o * _silu(z_ref[pl.ds(off, rb), :])).astype(y_ref.dtype)
        return carry

    lax.fori_loop(0, t // rb, epilogue, 0)


def _gdn(proj_big, gsm, conv_w, norm_w, batch, t):
    slots = N_DIR * min(CHUNK_GROUP, t // CHUNK)
    head_block = lambda base: pl.BlockSpec((t, HEAD_DIM), lambda b, h: (b, base + h))
    conv_block = lambda base: pl.BlockSpec((CONV_WIDTH, HEAD_DIM), lambda b, h: (0, base + h))
    return pl.pallas_call(
        _gdn_kernel,
        grid=(batch, HEADS),
        in_specs=[
            head_block(COL_GQ), head_block(COL_GK), head_block(COL_GV), head_block(COL_GZ),
            pl.BlockSpec((None, None, t, 4 * N_DIR), lambda b, h: (b, h, 0, 0)),
            conv_block(0), conv_block(HEADS), conv_block(2 * HEADS),
            pl.BlockSpec((1, HEAD_DIM), lambda b, h: (0, 0)),
        ],
        out_specs=pl.BlockSpec((t, HEAD_DIM), lambda b, h: (b, h)),
        out_shape=jax.ShapeDtypeStruct((batch * t, HEADS * HEAD_DIM), BF16),
        scratch_shapes=[
            pltpu.VMEM((t + 2 * CONV_HALO, HEAD_DIM), F32),
            pltpu.VMEM((t, HEAD_DIM), F32),
            pltpu.VMEM((t, HEAD_DIM), F32),
            pltpu.VMEM((t, HEAD_DIM), F32),
            pltpu.VMEM((t, HEAD_DIM), F32),
            pltpu.VMEM((slots * AQ_ROWS, HEAD_DIM), BF16),
            pltpu.VMEM((slots * HEAD_DIM, HEAD_DIM), F32),
            pltpu.VMEM((slots * CHUNK, HEAD_DIM), F32),
            pltpu.VMEM((slots * 8, LANES), F32),
        ],
        compiler_params=pltpu.CompilerParams(
            dimension_semantics=("parallel", "parallel"), vmem_limit_bytes=VMEM_LIMIT),
        name="gdn",
    )(proj_big, proj_big, proj_big, proj_big, gsm, conv_w, conv_w, conv_w, norm_w)


def _mlstm_kernel(q_ref, k_ref, v_ref, op_ref, z_ref, g_ref, cq_ref, ck_ref, nw_ref,
                  y_ref, pad_scr, q_scr, k_scr, o_scr, cf_scr, cb_scr):
    t = q_ref.shape[0]
    n_chunks = t // CHUNK

    _conv_silu_into(q_ref, cq_ref, pad_scr, q_scr, lambda y: y)
    _conv_silu_into(k_ref, ck_ref, pad_scr, k_scr, lambda y: y * (HEAD_DIM ** -0.5))

    o_scr[...] = jnp.zeros_like(o_scr)
    cf_scr[...] = jnp.zeros_like(cf_scr)
    cb_scr[...] = jnp.zeros_like(cb_scr)

    masks = _direction_masks()
    ones_col = (lax.broadcasted_iota(jnp.int32, (CHUNK, LANES), 1) == 0).astype(F32)

    def chunk_terms(chains):
        dirs = [d for _, d in chains]
        offs = [pl.multiple_of(c * CHUNK, CHUNK) for c, _ in chains]
        qc = [q_scr[pl.ds(off, CHUNK), :] for off in offs]
        kc = [k_scr[pl.ds(off, CHUNK), :] for off in offs]
        v_aug = [jnp.concatenate([v_ref[pl.ds(off, CHUNK), :], ones_col], axis=1) for off in offs]
        gt = [g_ref[pl.ds(off, CHUNK), :] for off in offs]

        igb = _each(lambda x, d: jnp.broadcast_to(x[:, 4 + d:5 + d], (CHUNK, LANES)), gt, dirs)
        bcb = _each(lambda x, d: _chunk_cumsum(masks[d][2], x[:, 6 + d:7 + d]), gt, dirs)
        qk = _each(_mm_nt, qc, kc)
        bc_rows = _each(_as_rows, bcb)
        ig_rows = _each(_as_rows, igb)
        d_log = _each(lambda b, br, ir, d: jnp.where(masks[d][0], b[:, :CHUNK] - br + ir, -jnp.inf),
                      bcb, bc_rows, ig_rows, dirs)
        m_intra = _each(lambda x: jnp.max(x, axis=1, keepdims=True), d_log)
        b_last = _each(lambda b, d: b[masks[d][3]:masks[d][3] + 1, :], bcb, dirs)
        a_end = _each(lambda bl, b, ig: bl - b + ig, b_last, bcb, igb)
        m_loc = _each(lambda x: jnp.max(x, axis=0, keepdims=True), a_end)
        k_end = _each(lambda k, a, m: k * jnp.exp(a - m), kc, a_end, m_loc)
        delta_n = _each(lambda x: jnp.sum(x, axis=0, keepdims=True), k_end)
        delta_c = _each(lambda x, v: _mm(x.T, v[:, :HEAD_DIM]), k_end, v_aug)
        intra = _each(lambda s, dl, m, v: _mm(s * jnp.exp(dl - m), v), qk, d_log, m_intra, v_aug)
        return [(qc[i], bcb[i], m_intra[i], b_last[i], m_loc[i], delta_c[i], delta_n[i], intra[i], offs[i])
                for i in range(len(chains))]

    def advance(state, terms):
        c_st, n_st, m_st = state
        qc, bcb, m_intra, b_last, m_loc, delta_c, delta_n, intra, off = terms
        m_inter = bcb + m_st
        m_t = jnp.maximum(m_inter, m_intra)
        w_inter = jnp.exp(m_inter - m_t)
        w_intra = jnp.exp(m_intra - m_t)
        inter = _mm(qc, c_st)
        num = w_inter * inter + w_intra * intra[:, :HEAD_DIM]
        den = (w_inter[:, :1] * jnp.sum(qc * n_st, axis=1, keepdims=True)
               + w_intra[:, :1] * intra[:, HEAD_DIM:HEAD_DIM + 1])
        o_scr[pl.ds(off, CHUNK), :] += num / jnp.maximum(jnp.abs(den), jnp.exp(-m_t))
        m_new = jnp.maximum(b_last + m_st, m_loc)
        s_old = jnp.exp(b_last + m_st - m_new)
        s_loc = jnp.exp(m_loc - m_new)
        return c_st * s_old + delta_c * s_loc, n_st * s_old + delta_n * s_loc, m_new

    group = min(CHUNK_GROUP, n_chunks)

    def step(i, carry):
        chains = []
        for j in range(group):
            chains += [(i * group + j, 0), (n_chunks - 1 - (i * group + j), 1)]
        terms = chunk_terms(chains)
        s_fwd = (cf_scr[...],) + tuple(carry[:2])
        s_bwd = (cb_scr[...],) + tuple(carry[2:])
        for j in range(group):
            s_fwd = advance(s_fwd, terms[2 * j])
            s_bwd = advance(s_bwd, terms[2 * j + 1])
        cf_scr[...] = s_fwd[0]
        cb_scr[...] = s_bwd[0]
        return s_fwd[1:] + s_bwd[1:]

    zero_row = jnp.zeros((1, LANES), F32)
    lax.fori_loop(0, n_chunks // group, step, (zero_row,) * 4)

    rb = min(CONV_ROWS, t)
    nw = nw_ref[...]

    def epilogue(r, carry):
        off = pl.multiple_of(r * rb, rb)
        hh = o_scr[pl.ds(off, rb), :]
        hc = hh - jnp.mean(hh, axis=-1, keepdims=True)
        hn = hc * lax.rsqrt(jnp.mean(hc * hc, axis=-1, keepdims=True) + NORM_EPS) * nw
        gate = jax.nn.sigmoid(op_ref[pl.ds(off, rb), :]) * _silu(z_ref[pl.ds(off, rb), :])
        y_ref[pl.ds(off, rb), :] = (hn * gate).astype(y_ref.dtype)
        return carry

    lax.fori_loop(0, t // rb, epilogue, 0)


def _mlstm(proj_big, gsm, conv_w, norm_w, batch, t):
    head_block = lambda base: pl.BlockSpec((t, HEAD_DIM), lambda b, h: (b, base + h))
    conv_block = lambda base: pl.BlockSpec((CONV_WIDTH, HEAD_DIM), lambda b, h: (0, base + h))
    return pl.pallas_call(
        _mlstm_kernel,
        grid=(batch, HEADS),
        in_specs=[
            head_block(COL_MQ), head_block(COL_MK), head_block(COL_MV), head_block(COL_MO),
            head_block(COL_MZ),
            pl.BlockSpec((None, None, t, 4 * N_DIR), lambda b, h: (b, h, 0, 0)),
            conv_block(0), conv_block(HEADS),
            pl.BlockSpec((1, HEAD_DIM), lambda b, h: (0, h)),
        ],
        out_specs=pl.BlockSpec((t, HEAD_DIM), lambda b, h: (b, h)),
        out_shape=jax.ShapeDtypeStruct((batch * t, HEADS * HEAD_DIM), BF16),
        scratch_shapes=[
            pltpu.VMEM((t + 2 * CONV_HALO, HEAD_DIM), F32),
            pltpu.VMEM((t, HEAD_DIM), F32),
            pltpu.VMEM((t, HEAD_DIM), F32),
            pltpu.VMEM((t, HEAD_DIM), F32),
            pltpu.VMEM((HEAD_DIM, HEAD_DIM), F32),
            pltpu.VMEM((HEAD_DIM, HEAD_DIM), F32),
        ],
        compiler_params=pltpu.CompilerParams(
            dimension_semantics=("parallel", "parallel"), vmem_limit_bytes=VMEM_LIMIT),
        name="mlstm",
    )(proj_big, proj_big, proj_big, proj_big, proj_big, gsm, conv_w, conv_w, norm_w)


def _out_proj_kernel(ya_ref, yb_ref, ga_ref, gb_ref, x_ref, wa_ref, wb_ref, wo_ref, gbias_ref, fw_ref,
                     o_ref):
    d = x_ref.shape[1]
    gbias = gbias_ref[...]
    a = jnp.dot(ya_ref[...], wa_ref[...], preferred_element_type=F32)
    b = jnp.dot(yb_ref[...], wb_ref[...], preferred_element_type=F32)
    merged = (jax.nn.sigmoid(ga_ref[...] + gbias[:, :d]) * a
              + jax.nn.sigmoid(gb_ref[...] + gbias[:, d:]) * b)
    xo = x_ref[...] + jnp.dot(merged.astype(BF16), wo_ref[...], preferred_element_type=F32)
    o_ref[...] = xo * lax.rsqrt(jnp.mean(xo * xo, axis=-1, keepdims=True) + NORM_EPS) * fw_ref[...]


def _out_proj(ya, yb, proj_big, x2, wa, wb, wo, gate_bias, final_w):
    m, d = x2.shape
    c = ya.shape[1]
    tm = min(256, m)
    const = lambda shape: pl.BlockSpec(shape, lambda i: (0, 0))
    return pl.pallas_call(
        _out_proj_kernel,
        grid=(m // tm,),
        in_specs=[
            pl.BlockSpec((tm, c), lambda i: (i, 0)),
            pl.BlockSpec((tm, c), lambda i: (i, 0)),
            pl.BlockSpec((tm, d), lambda i: (i, 0)),
            pl.BlockSpec((tm, d), lambda i: (i, 1)),
            pl.BlockSpec((tm, d), lambda i: (i, 0)),
            const((c, d)), const((c, d)), const((d, d)),
            const((1, 2 * d)), const((1, d)),
        ],
        out_specs=pl.BlockSpec((tm, d), lambda i: (i, 0)),
        out_shape=jax.ShapeDtypeStruct((m, d), F32),
        compiler_params=pltpu.CompilerParams(
            dimension_semantics=("parallel",), vmem_limit_bytes=VMEM_LIMIT),
        name="out_proj",
    )(ya, yb, proj_big, proj_big, x2, wa, wb, wo, gate_bias, final_w)


def kernel(x, w_in, conv_gdn, gdn_a_log, gdn_dt_bias, gdn_norm_w, conv_mlstm, mlstm_i_bias, mlstm_f_bias,
           mlstm_norm_w, gate_bias, w_branch_gdn, w_branch_mlstm, w_out, norm_w, final_norm_w):
    batch, t, d = x.shape
    depth = w_in.shape[0]
    assert depth == 1, "the final rmsnorm is fused into the single layer's output projection"
    key = HEADS * HEAD_DIM
    sizes = (3 * key, key, N_DIR * HEADS, N_DIR * HEADS, 2 * key, key, key, key, N_DIR * HEADS, N_DIR * HEADS,
             2 * d)
    offs = [0]
    for s in sizes:
        offs.append(offs[-1] + s)
    seg = lambda w, i: w[offs[i]:offs[i + 1], :]

    x2 = x.reshape(batch * t, d)
    for layer in range(depth):
        wt = jnp.swapaxes(w_in[layer], 0, 1)
        wt_small = jnp.concatenate([seg(wt, 2), seg(wt, 3), seg(wt, 8), seg(wt, 9)], axis=0)
        wt_small = jnp.pad(wt_small, ((0, LANES - wt_small.shape[0]), (0, 0))).astype(BF16)
        row_runs = ((0, offs[10]), (2 * d, offs[0]), (2 * d + 4 * key, offs[4]))
        width = N_DIR * HEADS
        zeros = jnp.zeros((width,), F32)
        gate_params = jnp.stack([
            jnp.concatenate([zeros, gdn_a_log[layer].reshape(-1), zeros, zeros]),
            jnp.concatenate([zeros, gdn_dt_bias[layer].reshape(-1), mlstm_i_bias[layer].reshape(-1),
                             mlstm_f_bias[layer].reshape(-1)])])
        gate_params = jnp.pad(gate_params, ((0, 0), (0, LANES - 4 * width)))
        proj_big, small = _in_proj(x2, norm_w[layer][None, :], wt.astype(BF16), wt_small, gate_params, row_runs)
        gsm = small[:, :4 * N_DIR * HEADS].reshape(batch, t, 4, N_DIR, HEADS)
        gsm = gsm.transpose(0, 4, 1, 2, 3).reshape(batch, HEADS, t, 4 * N_DIR)
        ya = _gdn(proj_big, gsm, conv_gdn[layer], gdn_norm_w[layer][None, :], batch, t)
        yb = _mlstm(proj_big, gsm, conv_mlstm[layer], mlstm_norm_w[layer][None, :], batch, t)
        x2 = _out_proj(ya, yb, proj_big, x2, w_branch_gdn[layer].astype(BF16),
                       w_branch_mlstm[layer].astype(BF16), w_out[layer].astype(BF16),
                       gate_bias[layer][None, :], final_norm_w[None, :])
    return x2.reshape(batch, t, d)
```

```python
import jax
import jax.numpy as jnp
from jax import lax
from jax.experimental import pallas as pl
from jax.experimental.pallas import tpu as pltpu

F32 = jnp.float32
BF16 = jnp.bfloat16

HEADS = 8
HEAD_DIM = 128
CHUNK = 64
CONV_WIDTH = 5
N_DIR = 2
NORM_EPS = 1e-6
LANES = 128
BF16_SUBLANES = 16
CONV_ROWS = 1024
CONV_HALO = 8
CHUNK_GROUP = 8
AQ_ROWS = HEAD_DIM + CHUNK
VMEM_LIMIT = 56 * 1024 * 1024

GATE_KINDS = GATE_BETA, GATE_G, GATE_I, GATE_F = tuple(range(4))

COL_GQ, COL_GK, COL_GV, COL_GZ = 32, 40, 48, 56
COL_MQ, COL_MK, COL_MV, COL_MO, COL_MZ = 64, 72, 80, 88, 96
BIG_COLS = 104 * LANES

_NT_DIMS = (((1,), (1,)), ((), ()))


def _mm(a, b):
    return jnp.dot(a.astype(BF16), b.astype(BF16), preferred_element_type=F32)


def _mm_nt(a, b):
    return lax.dot_general(a.astype(BF16), b.astype(BF16), _NT_DIMS, preferred_element_type=F32)


def _silu(x):
    return x * jax.nn.sigmoid(x)


def _in_proj_kernel(x_ref, nw_ref, wbig_ref, wsmall_ref, gpar_ref, big_ref, gates_ref, n_scr):
    @pl.when(pl.program_id(1) == 0)
    def _():
        x = x_ref[...]
        y = x * lax.rsqrt(jnp.mean(x * x, axis=-1, keepdims=True) + NORM_EPS) * nw_ref[...]
        n_scr[...] = y.astype(BF16)
        pre = lax.dot_general(n_scr[...], wsmall_ref[...], _NT_DIMS, preferred_element_type=F32)
        z = pre + gpar_ref[1:2, :]
        col = lax.broadcasted_iota(jnp.int32, z.shape, 1)
        width = N_DIR * HEADS
        log_decay = -jnp.exp(gpar_ref[0:1, :]) * jax.nn.softplus(z)
        gates_ref[...] = jnp.where(col < width, jax.nn.sigmoid(z),
                                   jnp.where(col < 2 * width, log_decay,
                                             jnp.where(col < 3 * width, z, jax.nn.log_sigmoid(z))))

    big_ref[...] = lax.dot_general(n_scr[...], wbig_ref[...], _NT_DIMS,
                                   preferred_element_type=F32).astype(big_ref.dtype)


def _in_proj(x2, norm_w, wt, wt_small, gate_params, row_runs):
    m, d = x2.shape
    tm = min(1024, m)
    tn = 1024
    assert all(col % tn == 0 and row % BF16_SUBLANES == 0 for col, row in row_runs)

    def source_row(j):
        row = jnp.int32(0)
        for col, first_row in row_runs:
            row = jnp.where(j >= col // tn, first_row + (j - col // tn) * tn, row)
        return pl.multiple_of(row, BF16_SUBLANES)

    return pl.pallas_call(
        _in_proj_kernel,
        grid=(m // tm, BIG_COLS // tn),
        in_specs=[
            pl.BlockSpec((tm, d), lambda i, j: (i, 0)),
            pl.BlockSpec((1, d), lambda i, j: (0, 0)),
            pl.BlockSpec((pl.Element(tn), pl.Element(d)), lambda i, j: (source_row(j), 0)),
            pl.BlockSpec((LANES, d), lambda i, j: (0, 0)),
            pl.BlockSpec((2, LANES), lambda i, j: (0, 0)),
        ],
        out_specs=[
            pl.BlockSpec((tm, tn), lambda i, j: (i, j)),
            pl.BlockSpec((tm, LANES), lambda i, j: (i, 0)),
        ],
        out_shape=[
            jax.ShapeDtypeStruct((m, BIG_COLS), BF16),
            jax.ShapeDtypeStruct((m, LANES), F32),
        ],
        scratch_shapes=[pltpu.VMEM((tm, d), BF16)],
        compiler_params=pltpu.CompilerParams(
            dimension_semantics=("parallel", "arbitrary"), vmem_limit_bytes=VMEM_LIMIT),
        name="in_proj",
    )(x2, norm_w, wt, wt_small, gate_params)


def _conv_silu_into(src_ref, w_ref, pad_scr, dst_scr, post):
    t = src_ref.shape[0]
    rb = min(CONV_ROWS, t)
    zeros = jnp.zeros((CONV_HALO, HEAD_DIM), F32)
    pad_scr[0:CONV_HALO, :] = zeros
    pad_scr[t + CONV_HALO:t + 2 * CONV_HALO, :] = zeros

    def copy(r, carry):
        off = pl.multiple_of(r * rb, rb)
        pad_scr[pl.ds(off + CONV_HALO, rb), :] = src_ref[pl.ds(off, rb), :].astype(F32)
        return carry

    lax.fori_loop(0, t // rb, copy, 0)
    w = w_ref[...]
    pad = (CONV_WIDTH - 1) // 2

    def body(r, carry):
        off = pl.multiple_of(r * rb, rb)
        acc = None
        for j in range(CONV_WIDTH):
            term = pad_scr[pl.ds(off + (CONV_HALO - pad + j), rb), :] * w[j:j + 1, :]
            acc = term if acc is None else acc + term
        dst_scr[pl.ds(off, rb), :] = post(_silu(acc)).astype(dst_scr.dtype)
        return carry

    lax.fori_loop(0, t // rb, body, 0)


def _direction_masks():
    row = lax.broadcasted_iota(jnp.int32, (CHUNK, CHUNK), 0)
    col = lax.broadcasted_iota(jnp.int32, (CHUNK, CHUNK), 1)
    fwd = (row >= col, row > col)
    bwd = (row <= col, row < col)
    out = []
    for (incl, strict), last in ((fwd, CHUNK - 1), (bwd, 0)):
        tri = incl.astype(BF16)
        out.append((incl, strict, jnp.concatenate([tri, tri, tri], axis=1), last))
    return out


def _cumsum_pieces(gates, kind, d):
    c = N_DIR * kind + d
    x = jnp.broadcast_to(gates[:, c:c + 1], (CHUNK, LANES))
    hi = x.astype(BF16)
    rest = x - hi.astype(F32)
    mid = rest.astype(BF16)
    lo = (rest - mid.astype(F32)).astype(BF16)
    return jnp.concatenate([hi, mid, lo], axis=0)


def _as_rows(col_bcast):
    return col_bcast.T[:CHUNK, :]


def _each(fn, *columns):
    return [fn(*args) for args in zip(*columns)]


def _interleave(*generators):
    live = list(generators)
    while live:
        for gen in list(live):
            try:
                next(gen)
            except StopIteration:
                live.remove(gen)


def _unit_triangular_inverses(lms, eye):
    inv = [eye - lm for lm in lms]
    power = _each(lambda x: _mm(x, x), lms)
    yield
    exponent = 2
    while 2 * exponent < CHUNK:
        both = _each(lambda p, x: _mm(jnp.concatenate([p, x], axis=0), x), inv, power)
        yield
        inv = _each(lambda p, b: p + b[:CHUNK], inv, both)
        power = [b[CHUNK:] for b in both]
        exponent *= 2
    inv = _each(lambda p, x: p + _mm(p, x), inv, power)
    yield
    return inv


def _mixers_kernel(gq_ref, gk_ref, gv_ref, gz_ref, mq_ref, mk_ref, mv_ref, mo_ref, mz_ref, g_ref,
                   cgq_ref, cgk_ref, cgv_ref, cmq_ref, cmk_ref, gnw_ref, mnw_ref,
                   ya_ref, yb_ref,
                   pad_scr, gq_scr, gk_scr, gv_scr, go_scr, mq_scr, mk_scr, mo_scr,
                   aq_pend, b_pend, o_pend, gam_pend, cf_scr, cb_scr):
    t = gq_ref.shape[0]
    n_chunks = t // CHUNK
    group = min(CHUNK_GROUP, n_chunks)
    n_groups = n_chunks // group

    def l2(y):
        return y * lax.rsqrt(jnp.sum(y * y, axis=-1, keepdims=True) + NORM_EPS)

    _conv_silu_into(gq_ref, cgq_ref, pad_scr, gq_scr, lambda y: l2(y) * (HEAD_DIM ** -0.5))
    _conv_silu_into(gk_ref, cgk_ref, pad_scr, gk_scr, l2)
    _conv_silu_into(gv_ref, cgv_ref, pad_scr, gv_scr, lambda y: y)
    _conv_silu_into(mq_ref, cmq_ref, pad_scr, mq_scr, lambda y: y)
    _conv_silu_into(mk_ref, cmk_ref, pad_scr, mk_scr, lambda y: y * (HEAD_DIM ** -0.5))

    go_scr[...] = jnp.zeros_like(go_scr)
    mo_scr[...] = jnp.zeros_like(mo_scr)
    cf_scr[...] = jnp.zeros_like(cf_scr)
    cb_scr[...] = jnp.zeros_like(cb_scr)

    masks = _direction_masks()
    eye = (lax.broadcasted_iota(jnp.int32, (CHUNK, CHUNK), 0)
           == lax.broadcasted_iota(jnp.int32, (CHUNK, CHUNK), 1)).astype(F32)
    ones_col = (lax.broadcasted_iota(jnp.int32, (CHUNK, LANES), 1) == 0).astype(BF16)

    def chains_of(i):
        chains = []
        for j in range(group):
            chains += [(i * group + j, 0), (n_chunks - 1 - (i * group + j), 1)]
        return chains

    def chunk_cumsum(gates, kind, d):
        return jnp.dot(masks[d][2], _cumsum_pieces(gates, kind, d), preferred_element_type=F32)


    def gdn_terms(i):
        chains = chains_of(i)
        dirs = [d for _, d in chains]
        offs = [pl.multiple_of(c * CHUNK, CHUNK) for c, _ in chains]
        q_bf = [gq_scr[pl.ds(off, CHUNK), :] for off in offs]
        k_bf = [gk_scr[pl.ds(off, CHUNK), :] for off in offs]
        kq = _each(lambda k, q: _mm_nt(jnp.concatenate([k, q], axis=0), k), k_bf, q_bf)
        yield
        gt = [g_ref[pl.ds(off, CHUNK), :] for off in offs]
        gcb = _each(lambda x, d: chunk_cumsum(x, GATE_G, d), gt, dirs)
        yield
        qc = [x.astype(F32) for x in q_bf]
        kc = [x.astype(F32) for x in k_bf]
        vc = [gv_scr[pl.ds(off, CHUNK), :].astype(F32) for off in offs]
        beta = _each(lambda x, d: x[:, N_DIR * GATE_BETA + d:N_DIR * GATE_BETA + d + 1], gt, dirs)
        gc_rows = _each(_as_rows, gcb)
        decay = _each(lambda g, r, d: jnp.exp(jnp.where(masks[d][0], g[:, :CHUNK] - r, -jnp.inf)),
                      gcb, gc_rows, dirs)
        lm = _each(lambda x, b, dec, d: jnp.where(masks[d][1], x[:CHUNK] * b * dec, 0.0), kq, beta, decay, dirs)
        attn = _each(lambda x, dec: x[CHUNK:] * dec, kq, decay)
        egc = _each(jnp.exp, gcb)
        rhs = _each(lambda v, k, b, e: jnp.concatenate([v * b, k * b * e], axis=1), vc, kc, beta, egc)
        inv = yield from _unit_triangular_inverses(lm, eye)
        sol = _each(_mm, inv, rhs)
        yield
        g_last = _each(lambda g, d: g[masks[d][3]:masks[d][3] + 1, :], gcb, dirs)
        k_dec_t = _each(lambda k, gl, g: (k * jnp.exp(gl - g)).T, kc, g_last, gcb)
        prod = _each(lambda a, kt, s: _mm(jnp.concatenate([a, kt], axis=0), s), attn, k_dec_t, sol)
        yield
        for slot in range(len(chains)):
            p = prod[slot]
            q_eff = qc[slot] * egc[slot] - p[:CHUNK, HEAD_DIM:]
            aq_pend[slot * AQ_ROWS:(slot + 1) * AQ_ROWS, :] = jnp.concatenate(
                [p[CHUNK:, HEAD_DIM:], q_eff], axis=0).astype(BF16)
            b_pend[slot * HEAD_DIM:(slot + 1) * HEAD_DIM, :] = p[CHUNK:, :HEAD_DIM]
            o_pend[slot * CHUNK:(slot + 1) * CHUNK, :] = p[:CHUNK, :HEAD_DIM]
            gam_pend[slot * 8:(slot + 1) * 8, :] = jnp.broadcast_to(jnp.exp(g_last[slot]), (8, LANES))

    def gdn_advance(state, slot, chunk):
        off = pl.multiple_of(chunk * CHUNK, CHUNK)
        r = jnp.dot(aq_pend[slot * AQ_ROWS:(slot + 1) * AQ_ROWS, :], state.astype(BF16),
                    preferred_element_type=F32)
        go_scr[pl.ds(off, CHUNK), :] += o_pend[slot * CHUNK:(slot + 1) * CHUNK, :] + r[HEAD_DIM:]
        return (state * gam_pend[slot * 8:slot * 8 + 1, :] - r[:HEAD_DIM]
                + b_pend[slot * HEAD_DIM:(slot + 1) * HEAD_DIM, :])

    def gdn_walk(i, states):
        for j in range(group):
            states[0] = gdn_advance(states[0], 2 * j, i * group + j)
            states[1] = gdn_advance(states[1], 2 * j + 1, n_chunks - 1 - (i * group + j))
            yield


    def mlstm_advance(state, terms):
        c_st, n_st, m_st = state
        qc, bcb, m_intra, b_last, m_loc, delta_c, delta_n, intra, off = terms
        m_inter = bcb + m_st
        m_t = jnp.maximum(m_inter, m_intra)
        w_inter = jnp.exp(m_inter - m_t)
        w_intra = jnp.exp(m_intra - m_t)
        inter = _mm(qc, c_st)
        num = w_inter * inter + w_intra * intra[:, :HEAD_DIM]
        den = (w_inter[:, :1] * jnp.sum(qc.astype(F32) * n_st, axis=1, keepdims=True)
               + w_intra[:, :1] * intra[:, HEAD_DIM:HEAD_DIM + 1])
        mo_scr[pl.ds(off, CHUNK), :] += num / jnp.maximum(jnp.abs(den), jnp.exp(-m_t))
        m_new = jnp.maximum(b_last + m_st, m_loc)
        s_old = jnp.exp(b_last + m_st - m_new)
        s_loc = jnp.exp(m_loc - m_new)
        return c_st * s_old + delta_c * s_loc, n_st * s_old + delta_n * s_loc, m_new

    def mlstm_group(i, rows):
        chains = chains_of(i)
        dirs = [d for _, d in chains]
        offs = [pl.multiple_of(c * CHUNK, CHUNK) for c, _ in chains]
        qc = [mq_scr[pl.ds(off, CHUNK), :] for off in offs]
        kc = [mk_scr[pl.ds(off, CHUNK), :] for off in offs]
        v_aug = [jnp.concatenate([mv_ref[pl.ds(off, CHUNK), :], ones_col], axis=1) for off in offs]
        gt = [g_ref[pl.ds(off, CHUNK), :] for off in offs]
        qk = _each(_mm_nt, qc, kc)
        yield
        igb = _each(lambda x, d: jnp.broadcast_to(
            x[:, N_DIR * GATE_I + d:N_DIR * GATE_I + d + 1], (CHUNK, LANES)), gt, dirs)
        bcb = _each(lambda x, d: chunk_cumsum(x, GATE_F, d), gt, dirs)
        yield
        bc_rows = _each(_as_rows, bcb)
        ig_rows = _each(_as_rows, igb)
        d_log = _each(lambda b, br, ir, d: jnp.where(masks[d][0], b[:, :CHUNK] - br + ir, -jnp.inf),
                      bcb, bc_rows, ig_rows, dirs)
        m_intra = _each(lambda x: jnp.max(x, axis=1, keepdims=True), d_log)
        b_last = _each(lambda b, d: b[masks[d][3]:masks[d][3] + 1, :], bcb, dirs)
        a_end = _each(lambda bl, b, ig: bl - b + ig, b_last, bcb, igb)
        m_loc = _each(lambda x: jnp.max(x, axis=0, keepdims=True), a_end)
        k_end = _each(lambda k, a, m: k.astype(F32) * jnp.exp(a - m), kc, a_end, m_loc)
        delta_n = _each(lambda x: jnp.sum(x, axis=0, keepdims=True), k_end)
        delta_c = _each(lambda x, v: _mm(x.T, v[:, :HEAD_DIM]), k_end, v_aug)
        yield
        intra = _each(lambda s, dl, m, v: _mm(s * jnp.exp(dl - m), v), qk, d_log, m_intra, v_aug)
        yield
        terms = [(qc[k], bcb[k], m_intra[k], b_last[k], m_loc[k], delta_c[k], delta_n[k], intra[k], offs[k])
                 for k in range(len(chains))]
        s_fwd = (cf_scr[...], rows[0], rows[1])
        s_bwd = (cb_scr[...], rows[2], rows[3])
        for j in range(group):
            s_fwd = mlstm_advance(s_fwd, terms[2 * j])
            s_bwd = mlstm_advance(s_bwd, terms[2 * j + 1])
            yield
        cf_scr[...] = s_fwd[0]
        cb_scr[...] = s_bwd[0]
        rows[:] = [s_fwd[1], s_fwd[2], s_bwd[1], s_bwd[2]]


    _interleave(gdn_terms(0))

    def step(i, carry):
        states = list(carry[:2])
        rows = list(carry[2:])
        _interleave(gdn_terms(i), gdn_walk(i - 1, states), mlstm_group(i - 1, rows))
        return tuple(states) + tuple(rows)

    zero_state = jnp.zeros((HEAD_DIM, HEAD_DIM), F32)
    zero_row = jnp.zeros((1, LANES), F32)
    carry = lax.fori_loop(1, n_groups, step, (zero_state, zero_state) + (zero_row,) * 4)
    states = list(carry[:2])
    rows = list(carry[2:])
    _interleave(gdn_walk(n_groups - 1, states), mlstm_group(n_groups - 1, rows))


    rb = min(CONV_ROWS, t)
    gnw = gnw_ref[...]
    mnw = mnw_ref[...]

    def epilogue(r, carry):
        off = pl.multiple_of(r * rb, rb)
        o = go_scr[pl.ds(off, rb), :]
        o = o * lax.rsqrt(jnp.mean(o * o, axis=-1, keepdims=True) + NORM_EPS) * gnw
        ya_ref[pl.ds(off, rb), :] = (o * _silu(gz_ref[pl.ds(off, rb), :].astype(F32))).astype(ya_ref.dtype)
        hh = mo_scr[pl.ds(off, rb), :]
        hc = hh - jnp.mean(hh, axis=-1, keepdims=True)
        hn = hc * lax.rsqrt(jnp.mean(hc * hc, axis=-1, keepdims=True) + NORM_EPS) * mnw
        gate = (jax.nn.sigmoid(mo_ref[pl.ds(off, rb), :].astype(F32))
                * _silu(mz_ref[pl.ds(off, rb), :].astype(F32)))
        yb_ref[pl.ds(off, rb), :] = (hn * gate).astype(yb_ref.dtype)
        return carry

    lax.fori_loop(0, t // rb, epilogue, 0)


def _mixers(proj_big, gsm, conv_gdn, conv_mlstm, gdn_norm_w, mlstm_norm_w, batch, t):
    slots = N_DIR * min(CHUNK_GROUP, t // CHUNK)
    head_block = lambda base: pl.BlockSpec((t, HEAD_DIM), lambda b, h: (b, base + h))
    conv_block = lambda base: pl.BlockSpec((CONV_WIDTH, HEAD_DIM), lambda b, h: (0, base + h))
    seq = lambda dtype: pltpu.VMEM((t, HEAD_DIM), dtype)
    out_spec = pl.BlockSpec((t, HEAD_DIM), lambda b, h: (b, h))
    out_shape = jax.ShapeDtypeStruct((batch * t, HEADS * HEAD_DIM), BF16)
    return pl.pallas_call(
        _mixers_kernel,
        grid=(batch, HEADS),
        in_specs=[
            head_block(COL_GQ), head_block(COL_GK), head_block(COL_GV), head_block(COL_GZ),
            head_block(COL_MQ), head_block(COL_MK), head_block(COL_MV), head_block(COL_MO),
            head_block(COL_MZ),
            pl.BlockSpec((None, None, t, len(GATE_KINDS) * N_DIR), lambda b, h: (b, h, 0, 0)),
            conv_block(0), conv_block(HEADS), conv_block(2 * HEADS),
            conv_block(0), conv_block(HEADS),
            pl.BlockSpec((1, HEAD_DIM), lambda b, h: (0, 0)),
            pl.BlockSpec((1, HEAD_DIM), lambda b, h: (0, h)),
        ],
        out_specs=[out_spec, out_spec],
        out_shape=[out_shape, out_shape],
        scratch_shapes=[
            pltpu.VMEM((t + 2 * CONV_HALO, HEAD_DIM), F32),
            seq(BF16), seq(BF16), seq(BF16), seq(F32),
            seq(BF16), seq(BF16), seq(F32),
            pltpu.VMEM((slots * AQ_ROWS, HEAD_DIM), BF16),
            pltpu.VMEM((slots * HEAD_DIM, HEAD_DIM), F32),
            pltpu.VMEM((slots * CHUNK, HEAD_DIM), F32),
            pltpu.VMEM((slots * 8, LANES), F32),
            pltpu.VMEM((HEAD_DIM, HEAD_DIM), F32),
            pltpu.VMEM((HEAD_DIM, HEAD_DIM), F32),
        ],
        compiler_params=pltpu.CompilerParams(
            dimension_semantics=("parallel", "parallel"), vmem_limit_bytes=VMEM_LIMIT),
        name="mixers",
    )(proj_big, proj_big, proj_big, proj_big, proj_big, proj_big, proj_big, proj_big, proj_big, gsm,
      conv_gdn, conv_gdn, conv_gdn, conv_mlstm, conv_mlstm, gdn_norm_w, mlstm_norm_w)


def _out_proj_kernel(ya_ref, yb_ref, ga_ref, gb_ref, x_ref, wa_ref, wb_ref, wo_ref, gbias_ref, fw_ref,
                     o_ref):
    d = x_ref.shape[1]
    gbias = gbias_ref[...]
    a = jnp.dot(ya_ref[...], wa_ref[...], preferred_element_type=F32)
    b = jnp.dot(yb_ref[...], wb_ref[...], preferred_element_type=F32)
    merged = (jax.nn.sigmoid(ga_ref[...].astype(F32) + gbias[:, :d]) * a
              + jax.nn.sigmoid(gb_ref[...].astype(F32) + gbias[:, d:]) * b)
    xo = x_ref[...] + jnp.dot(merged.astype(BF16), wo_ref[...], preferred_element_type=F32)
    o_ref[...] = xo * lax.rsqrt(jnp.mean(xo * xo, axis=-1, keepdims=True) + NORM_EPS) * fw_ref[...]


def _out_proj(ya, yb, proj_big, x2, wa, wb, wo, gate_bias, final_w):
    m, d = x2.shape
    c = ya.shape[1]
    tm = min(256, m)
    const = lambda shape: pl.BlockSpec(shape, lambda i: (0, 0))
    return pl.pallas_call(
        _out_proj_kernel,
        grid=(m // tm,),
        in_specs=[
            pl.BlockSpec((tm, c), lambda i: (i, 0)),
            pl.BlockSpec((tm, c), lambda i: (i, 0)),
            pl.BlockSpec((tm, d), lambda i: (i, 0)),
            pl.BlockSpec((tm, d), lambda i: (i, 1)),
            pl.BlockSpec((tm, d), lambda i: (i, 0)),
            const((c, d)), const((c, d)), const((d, d)),
            const((1, 2 * d)), const((1, d)),
        ],
        out_specs=pl.BlockSpec((tm, d), lambda i: (i, 0)),
        out_shape=jax.ShapeDtypeStruct((m, d), F32),
        compiler_params=pltpu.CompilerParams(
            dimension_semantics=("parallel",), vmem_limit_bytes=VMEM_LIMIT),
        name="out_proj",
    )(ya, yb, proj_big, proj_big, x2, wa, wb, wo, gate_bias, final_w)


def kernel(x, w_in, conv_gdn, gdn_a_log, gdn_dt_bias, gdn_norm_w, conv_mlstm, mlstm_i_bias, mlstm_f_bias,
           mlstm_norm_w, gate_bias, w_branch_gdn, w_branch_mlstm, w_out, norm_w, final_norm_w):
    batch, t, d = x.shape
    depth = w_in.shape[0]
    assert depth == 1, "the final rmsnorm is fused into the single layer's output projection"
    key = HEADS * HEAD_DIM
    sizes = (3 * key, key, N_DIR * HEADS, N_DIR * HEADS, 2 * key, key, key, key, N_DIR * HEADS, N_DIR * HEADS,
             2 * d)
    offs = [0]
    for s in sizes:
        offs.append(offs[-1] + s)
    seg = lambda w, i: w[offs[i]:offs[i + 1], :]

    x2 = x.reshape(batch * t, d)
    for layer in range(depth):
        wt = jnp.swapaxes(w_in[layer], 0, 1)
        wt_small = jnp.concatenate([seg(wt, 2), seg(wt, 3), seg(wt, 8), seg(wt, 9)], axis=0)
        wt_small = jnp.pad(wt_small, ((0, LANES - wt_small.shape[0]), (0, 0))).astype(BF16)
        row_runs = ((0, offs[10]), (2 * d, offs[0]), (2 * d + 4 * key, offs[4]))
        width = N_DIR * HEADS
        zeros = jnp.zeros((width,), F32)
        gate_params = jnp.stack([
            jnp.concatenate([zeros, gdn_a_log[layer].reshape(-1), zeros, zeros]),
            jnp.concatenate([zeros, gdn_dt_bias[layer].reshape(-1), mlstm_i_bias[layer].reshape(-1),
                             mlstm_f_bias[layer].reshape(-1)])])
        gate_params = jnp.pad(gate_params, ((0, 0), (0, LANES - 4 * width)))
        proj_big, small = _in_proj(x2, norm_w[layer][None, :], wt.astype(BF16), wt_small, gate_params, row_runs)
        gsm = small[:, :len(GATE_KINDS) * N_DIR * HEADS].reshape(batch, t, len(GATE_KINDS), N_DIR, HEADS)
        gsm = gsm.transpose(0, 4, 1, 2, 3).reshape(batch, HEADS, t, len(GATE_KINDS) * N_DIR)
        ya, yb = _mixers(proj_big, gsm, conv_gdn[layer], conv_mlstm[layer], gdn_norm_w[layer][None, :],
                         mlstm_norm_w[layer][None, :], batch, t)
        x2 = _out_proj(ya, yb, proj_big, x2, w_branch_gdn[layer].astype(BF16),
                       w_branch_mlstm[layer].astype(BF16), w_out[layer].astype(BF16),
                       gate_bias[layer][None, :], final_norm_w[None, :])
    return x2.reshape(batch, t, d)
```

```python
import jax
import jax.numpy as jnp
from jax import lax
from jax.experimental import pallas as pl
from jax.experimental.pallas import tpu as pltpu

F32 = jnp.float32
BF16 = jnp.bfloat16

HEADS = 8
HEAD_DIM = 128
CHUNK = 64
CONV_WIDTH = 5
N_DIR = 2
NORM_EPS = 1e-6
LANES = 128
BF16_SUBLANES = 16
CONV_ROWS = 1024
CONV_HALO = 8
CHUNK_GROUP = 8
AQ_ROWS = HEAD_DIM + CHUNK
VMEM_LIMIT = 56 * 1024 * 1024

GATE_KINDS = GATE_BETA, GATE_G, GATE_I, GATE_F = tuple(range(4))

COL_GQ, COL_GK, COL_GV, COL_GZ = 32, 40, 48, 56
COL_MQ, COL_MK, COL_MV, COL_MO, COL_MZ = 64, 72, 80, 88, 96
BIG_COLS = 104 * LANES

_NT_DIMS = (((1,), (1,)), ((), ()))


def _mm(a, b):
    return jnp.dot(a.astype(BF16), b.astype(BF16), preferred_element_type=F32)


def _mm_nt(a, b):
    return lax.dot_general(a.astype(BF16), b.astype(BF16), _NT_DIMS, preferred_element_type=F32)


def _silu(x):
    return x * jax.nn.sigmoid(x)


def _in_proj_kernel(x_ref, nw_ref, wbig_ref, wsmall_ref, gpar_ref, big_ref, gates_ref, n_scr):
    @pl.when(pl.program_id(1) == 0)
    def _():
        x = x_ref[...]
        y = x * lax.rsqrt(jnp.mean(x * x, axis=-1, keepdims=True) + NORM_EPS) * nw_ref[...]
        n_scr[...] = y.astype(BF16)
        pre = lax.dot_general(n_scr[...], wsmall_ref[...].astype(BF16), _NT_DIMS, preferred_element_type=F32)
        z = pre + gpar_ref[1:2, :]
        col = lax.broadcasted_iota(jnp.int32, z.shape, 1)
        per_head = len(GATE_KINDS) * N_DIR
        kind = lax.shift_right_logical(jnp.bitwise_and(col, per_head - 1), 1)
        log_decay = -jnp.exp(gpar_ref[0:1, :]) * jax.nn.softplus(z)
        act = jnp.where(kind == GATE_BETA, jax.nn.sigmoid(z),
                        jnp.where(kind == GATE_G, log_decay,
                                  jnp.where(kind == GATE_I, z, jax.nn.log_sigmoid(z))))
        for h in range(HEADS):
            gates_ref[h] = act[:, h * per_head:(h + 1) * per_head]

    big_ref[...] = lax.dot_general(n_scr[...], wbig_ref[...].astype(BF16), _NT_DIMS,
                                   preferred_element_type=F32).astype(big_ref.dtype)


def _in_proj(x2, norm_w, wt, wt_small, gate_params, row_runs, t):
    m, d = x2.shape
    tm = min(1024, t)
    tn = 1024
    per_head = len(GATE_KINDS) * N_DIR
    blocks_per_seq = t // tm
    assert N_DIR == 2 and t % tm == 0
    assert all(col % tn == 0 and row % BF16_SUBLANES == 0 for col, row in row_runs)

    def source_row(j):
        row = jnp.int32(0)
        for col, first_row in row_runs:
            row = jnp.where(j >= col // tn, first_row + (j - col // tn) * tn, row)
        return pl.multiple_of(row, BF16_SUBLANES)

    return pl.pallas_call(
        _in_proj_kernel,
        grid=(m // tm, BIG_COLS // tn),
        in_specs=[
            pl.BlockSpec((tm, d), lambda i, j: (i, 0)),
            pl.BlockSpec((1, d), lambda i, j: (0, 0)),
            pl.BlockSpec((pl.Element(tn), pl.Element(d)), lambda i, j: (source_row(j), 0)),
            pl.BlockSpec((LANES, d), lambda i, j: (0, 0)),
            pl.BlockSpec((2, LANES), lambda i, j: (0, 0)),
        ],
        out_specs=[
            pl.BlockSpec((tm, tn), lambda i, j: (i, j)),
            pl.BlockSpec((None, HEADS, tm, per_head),
                         lambda i, j: (i // blocks_per_seq, 0, i % blocks_per_seq, 0)),
        ],
        out_shape=[
            jax.ShapeDtypeStruct((m, BIG_COLS), BF16),
            jax.ShapeDtypeStruct((m // t, HEADS, t, per_head), F32),
        ],
        scratch_shapes=[pltpu.VMEM((tm, d), BF16)],
        compiler_params=pltpu.CompilerParams(
            dimension_semantics=("parallel", "arbitrary"), vmem_limit_bytes=VMEM_LIMIT),
        name="in_proj",
    )(x2, norm_w, wt, wt_small, gate_params)


def _conv_silu_into(src_ref, w_ref, pad_scr, dst_scr, post):
    t = src_ref.shape[0]
    rb = min(CONV_ROWS, t)
    zeros = jnp.zeros((CONV_HALO, HEAD_DIM), F32)
    pad_scr[0:CONV_HALO, :] = zeros
    pad_scr[t + CONV_HALO:t + 2 * CONV_HALO, :] = zeros

    def copy(r, carry):
        off = pl.multiple_of(r * rb, rb)
        pad_scr[pl.ds(off + CONV_HALO, rb), :] = src_ref[pl.ds(off, rb), :].astype(F32)
        return carry

    lax.fori_loop(0, t // rb, copy, 0)
    w = w_ref[...]
    pad = (CONV_WIDTH - 1) // 2

    def body(r, carry):
        off = pl.multiple_of(r * rb, rb)
        acc = None
        for j in range(CONV_WIDTH):
            term = pad_scr[pl.ds(off + (CONV_HALO - pad + j), rb), :] * w[j:j + 1, :]
            acc = term if acc is None else acc + term
        dst_scr[pl.ds(off, rb), :] = post(_silu(acc)).astype(dst_scr.dtype)
        return carry

    lax.fori_loop(0, t // rb, body, 0)


def _direction_masks():
    row = lax.broadcasted_iota(jnp.int32, (CHUNK, CHUNK), 0)
    col = lax.broadcasted_iota(jnp.int32, (CHUNK, CHUNK), 1)
    fwd = (row >= col, row > col)
    bwd = (row <= col, row < col)
    out = []
    for (incl, strict), last in ((fwd, CHUNK - 1), (bwd, 0)):
        tri = incl.astype(BF16)
        out.append((incl, strict, jnp.concatenate([tri, tri, tri], axis=1), last))
    return out


def _cumsum_pieces(gates, kind, d):
    c = N_DIR * kind + d
    x = jnp.broadcast_to(gates[:, c:c + 1], (CHUNK, LANES))
    hi = x.astype(BF16)
    rest = x - hi.astype(F32)
    mid = rest.astype(BF16)
    lo = (rest - mid.astype(F32)).astype(BF16)
    return jnp.concatenate([hi, mid, lo], axis=0)


def _as_rows(col_bcast):
    return col_bcast.T[:CHUNK, :]


def _each(fn, *columns):
    return [fn(*args) for args in zip(*columns)]


def _interleave(*generators):
    live = list(generators)
    while live:
        for gen in list(live):
            try:
                next(gen)
            except StopIteration:
                live.remove(gen)


def _unit_triangular_inverses(lms, eye):
    inv = [eye - lm for lm in lms]
    power = _each(lambda x: _mm(x, x), lms)
    yield
    exponent = 2
    while 2 * exponent < CHUNK:
        both = _each(lambda p, x: _mm(jnp.concatenate([p, x], axis=0), x), inv, power)
        yield
        inv = _each(lambda p, b: p + b[:CHUNK], inv, both)
        power = [b[CHUNK:] for b in both]
        exponent *= 2
    inv = _each(lambda p, x: p + _mm(p, x), inv, power)
    yield
    return inv


def _mixers_kernel(gq_ref, gk_ref, gv_ref, gz_ref, mq_ref, mk_ref, mv_ref, mo_ref, mz_ref, g_ref,
                   cgq_ref, cgk_ref, cgv_ref, cmq_ref, cmk_ref, gnw_ref, mnw_ref,
                   ya_ref, yb_ref,
                   pad_scr, gq_scr, gk_scr, gv_scr, go_scr, mq_scr, mk_scr, mo_scr,
                   aq_pend, b_pend, o_pend, gam_pend, cf_scr, cb_scr):
    t = gq_ref.shape[0]
    n_chunks = t // CHUNK
    group = min(CHUNK_GROUP, n_chunks)
    n_groups = n_chunks // group

    def l2(y):
        return y * lax.rsqrt(jnp.sum(y * y, axis=-1, keepdims=True) + NORM_EPS)

    _conv_silu_into(gq_ref, cgq_ref, pad_scr, gq_scr, lambda y: l2(y) * (HEAD_DIM ** -0.5))
    _conv_silu_into(gk_ref, cgk_ref, pad_scr, gk_scr, l2)
    _conv_silu_into(gv_ref, cgv_ref, pad_scr, gv_scr, lambda y: y)
    _conv_silu_into(mq_ref, cmq_ref, pad_scr, mq_scr, lambda y: y)
    _conv_silu_into(mk_ref, cmk_ref, pad_scr, mk_scr, lambda y: y * (HEAD_DIM ** -0.5))

    go_scr[...] = jnp.zeros_like(go_scr)
    mo_scr[...] = jnp.zeros_like(mo_scr)
    cf_scr[...] = jnp.zeros_like(cf_scr)
    cb_scr[...] = jnp.zeros_like(cb_scr)

    masks = _direction_masks()
    eye = (lax.broadcasted_iota(jnp.int32, (CHUNK, CHUNK), 0)
           == lax.broadcasted_iota(jnp.int32, (CHUNK, CHUNK), 1)).astype(F32)
    ones_col = (lax.broadcasted_iota(jnp.int32, (CHUNK, LANES), 1) == 0).astype(BF16)

    def chains_of(i):
        chains = []
        for j in range(group):
            chains += [(i * group + j, 0), (n_chunks - 1 - (i * group + j), 1)]
        return chains

    def chunk_cumsum(gates, kind, d):
        return jnp.dot(masks[d][2], _cumsum_pieces(gates, kind, d), preferred_element_type=F32)


    def gdn_terms(i):
        chains = chains_of(i)
        dirs = [d for _, d in chains]
        offs = [pl.multiple_of(c * CHUNK, CHUNK) for c, _ in chains]
        q_bf = [gq_scr[pl.ds(off, CHUNK), :] for off in offs]
        k_bf = [gk_scr[pl.ds(off, CHUNK), :] for off in offs]
        kq = _each(lambda k, q: _mm_nt(jnp.concatenate([k, q], axis=0), k), k_bf, q_bf)
        yield
        gt = [g_ref[pl.ds(off, CHUNK), :] for off in offs]
        gcb = _each(lambda x, d: chunk_cumsum(x, GATE_G, d), gt, dirs)
        yield
        qc = [x.astype(F32) for x in q_bf]
        kc = [x.astype(F32) for x in k_bf]
        vc = [gv_scr[pl.ds(off, CHUNK), :].astype(F32) for off in offs]
        beta = _each(lambda x, d: x[:, N_DIR * GATE_BETA + d:N_DIR * GATE_BETA + d + 1], gt, dirs)
        gc_rows = _each(_as_rows, gcb)
        decay = _each(lambda g, r, d: jnp.exp(jnp.where(masks[d][0], g[:, :CHUNK] - r, -jnp.inf)),
                      gcb, gc_rows, dirs)
        lm = _each(lambda x, b, dec, d: jnp.where(masks[d][1], x[:CHUNK] * b * dec, 0.0), kq, beta, decay, dirs)
        attn = _each(lambda x, dec: x[CHUNK:] * dec, kq, decay)
        egc = _each(jnp.exp, gcb)
        rhs = _each(lambda v, k, b, e: jnp.concatenate([v * b, k * b * e], axis=1), vc, kc, beta, egc)
        inv = yield from _unit_triangular_inverses(lm, eye)
        sol = _each(_mm, inv, rhs)
        yield
        g_last = _each(lambda g, d: g[masks[d][3]:masks[d][3] + 1, :], gcb, dirs)
        k_dec_t = _each(lambda k, gl, g: (k * jnp.exp(gl - g)).T, kc, g_last, gcb)
        prod = _each(lambda a, kt, s: _mm(jnp.concatenate([a, kt], axis=0), s), attn, k_dec_t, sol)
        yield
        for slot in range(len(chains)):
            p = prod[slot]
            q_eff = qc[slot] * egc[slot] - p[:CHUNK, HEAD_DIM:]
            aq_pend[slot * AQ_ROWS:(slot + 1) * AQ_ROWS, :] = jnp.concatenate(
                [p[CHUNK:, HEAD_DIM:], q_eff], axis=0).astype(BF16)
            b_pend[slot * HEAD_DIM:(slot + 1) * HEAD_DIM, :] = p[CHUNK:, :HEAD_DIM]
            o_pend[slot * CHUNK:(slot + 1) * CHUNK, :] = p[:CHUNK, :HEAD_DIM]
            gam_pend[slot * 8:(slot + 1) * 8, :] = jnp.broadcast_to(jnp.exp(g_last[slot]), (8, LANES))

    def gdn_advance(state, slot, chunk):
        off = pl.multiple_of(chunk * CHUNK, CHUNK)
        r = jnp.dot(aq_pend[slot * AQ_ROWS:(slot + 1) * AQ_ROWS, :], state.astype(BF16),
                    preferred_element_type=F32)
        go_scr[pl.ds(off, CHUNK), :] += o_pend[slot * CHUNK:(slot + 1) * CHUNK, :] + r[HEAD_DIM:]
        return (state * gam_pend[slot * 8:slot * 8 + 1, :] - r[:HEAD_DIM]
                + b_pend[slot * HEAD_DIM:(slot + 1) * HEAD_DIM, :])

    def gdn_walk(i, states):
        for j in range(group):
            states[0] = gdn_advance(states[0], 2 * j, i * group + j)
            states[1] = gdn_advance(states[1], 2 * j + 1, n_chunks - 1 - (i * group + j))
            yield


    def mlstm_advance(state, terms):
        c_st, n_st, m_st = state
        qc, bcb, m_intra, b_last, m_loc, delta_c, delta_n, intra, off = terms
        m_inter = bcb + m_st
        m_t = jnp.maximum(m_inter, m_intra)
        w_inter = jnp.exp(m_inter - m_t)
        w_intra = jnp.exp(m_intra - m_t)
        inter = _mm(qc, c_st)
        num = w_inter * inter + w_intra * intra[:, :HEAD_DIM]
        den = (w_inter[:, :1] * jnp.sum(qc.astype(F32) * n_st, axis=1, keepdims=True)
               + w_intra[:, :1] * intra[:, HEAD_DIM:HEAD_DIM + 1])
        mo_scr[pl.ds(off, CHUNK), :] += num / jnp.maximum(jnp.abs(den), jnp.exp(-m_t))
        m_new = jnp.maximum(b_last + m_st, m_loc)
        s_old = jnp.exp(b_last + m_st - m_new)
        s_loc = jnp.exp(m_loc - m_new)
        return c_st * s_old + delta_c * s_loc, n_st * s_old + delta_n * s_loc, m_new

    def mlstm_group(i, rows):
        chains = chains_of(i)
        dirs = [d for _, d in chains]
        offs = [pl.multiple_of(c * CHUNK, CHUNK) for c, _ in chains]
        qc = [mq_scr[pl.ds(off, CHUNK), :] for off in offs]
        kc = [mk_scr[pl.ds(off, CHUNK), :] for off in offs]
        v_aug = [jnp.concatenate([mv_ref[pl.ds(off, CHUNK), :], ones_col], axis=1) for off in offs]
        gt = [g_ref[pl.ds(off, CHUNK), :] for off in offs]
        qk = _each(_mm_nt, qc, kc)
        yield
        igb = _each(lambda x, d: jnp.broadcast_to(
            x[:, N_DIR * GATE_I + d:N_DIR * GATE_I + d + 1], (CHUNK, LANES)), gt, dirs)
        bcb = _each(lambda x, d: chunk_cumsum(x, GATE_F, d), gt, dirs)
        yield
        bc_rows = _each(_as_rows, bcb)
        ig_rows = _each(_as_rows, igb)
        d_log = _each(lambda b, br, ir, d: jnp.where(masks[d][0], b[:, :CHUNK] - br + ir, -jnp.inf),
                      bcb, bc_rows, ig_rows, dirs)
        m_intra = _each(lambda x: jnp.max(x, axis=1, keepdims=True), d_log)
        b_last = _each(lambda b, d: b[masks[d][3]:masks[d][3] + 1, :], bcb, dirs)
        a_end = _each(lambda bl, b, ig: bl - b + ig, b_last, bcb, igb)
        m_loc = _each(lambda x: jnp.max(x, axis=0, keepdims=True), a_end)
        k_end = _each(lambda k, a, m: k.astype(F32) * jnp.exp(a - m), kc, a_end, m_loc)
        delta_n = _each(lambda x: jnp.sum(x, axis=0, keepdims=True), k_end)
        delta_c = _each(lambda x, v: _mm(x.T, v[:, :HEAD_DIM]), k_end, v_aug)
        yield
        intra = _each(lambda s, dl, m, v: _mm(s * jnp.exp(dl - m), v), qk, d_log, m_intra, v_aug)
        yield
        terms = [(qc[k], bcb[k], m_intra[k], b_last[k], m_loc[k], delta_c[k], delta_n[k], intra[k], offs[k])
                 for k in range(len(chains))]
        s_fwd = (cf_scr[...], rows[0], rows[1])
        s_bwd = (cb_scr[...], rows[2], rows[3])
        for j in range(group):
            s_fwd = mlstm_advance(s_fwd, terms[2 * j])
            s_bwd = mlstm_advance(s_bwd, terms[2 * j + 1])
            yield
        cf_scr[...] = s_fwd[0]
        cb_scr[...] = s_bwd[0]
        rows[:] = [s_fwd[1], s_fwd[2], s_bwd[1], s_bwd[2]]


    _interleave(gdn_terms(0))

    def step(i, carry):
        states = list(carry[:2])
        rows = list(carry[2:])
        _interleave(gdn_terms(i), gdn_walk(i - 1, states), mlstm_group(i - 1, rows))
        return tuple(states) + tuple(rows)

    zero_state = jnp.zeros((HEAD_DIM, HEAD_DIM), F32)
    zero_row = jnp.zeros((1, LANES), F32)
    carry = lax.fori_loop(1, n_groups, step, (zero_state, zero_state) + (zero_row,) * 4)
    states = list(carry[:2])
    rows = list(carry[2:])
    _interleave(gdn_walk(n_groups - 1, states), mlstm_group(n_groups - 1, rows))


    rb = min(CONV_ROWS, t)
    gnw = gnw_ref[...]
    mnw = mnw_ref[...]

    def epilogue(r, carry):
        off = pl.multiple_of(r * rb, rb)
        o = go_scr[pl.ds(off, rb), :]
        o = o * lax.rsqrt(jnp.mean(o * o, axis=-1, keepdims=True) + NORM_EPS) * gnw
        ya_ref[pl.ds(off, rb), :] = (o * _silu(gz_ref[pl.ds(off, rb), :].astype(F32))).astype(ya_ref.dtype)
        hh = mo_scr[pl.ds(off, rb), :]
        hc = hh - jnp.mean(hh, axis=-1, keepdims=True)
        hn = hc * lax.rsqrt(jnp.mean(hc * hc, axis=-1, keepdims=True) + NORM_EPS) * mnw
        gate = (jax.nn.sigmoid(mo_ref[pl.ds(off, rb), :].astype(F32))
                * _silu(mz_ref[pl.ds(off, rb), :].astype(F32)))
        yb_ref[pl.ds(off, rb), :] = (hn * gate).astype(yb_ref.dtype)
        return carry

    lax.fori_loop(0, t // rb, epilogue, 0)


def _mixers(proj_big, gsm, conv_gdn, conv_mlstm, gdn_norm_w, mlstm_norm_w, batch, t):
    slots = N_DIR * min(CHUNK_GROUP, t // CHUNK)
    head_block = lambda base: pl.BlockSpec((t, HEAD_DIM), lambda b, h: (b, base + h))
    conv_block = lambda base: pl.BlockSpec((CONV_WIDTH, HEAD_DIM), lambda b, h: (0, base + h))
    seq = lambda dtype: pltpu.VMEM((t, HEAD_DIM), dtype)
    out_spec = pl.BlockSpec((t, HEAD_DIM), lambda b, h: (b, h))
    out_shape = jax.ShapeDtypeStruct((batch * t, HEADS * HEAD_DIM), BF16)
    return pl.pallas_call(
        _mixers_kernel,
        grid=(batch, HEADS),
        in_specs=[
            head_block(COL_GQ), head_block(COL_GK), head_block(COL_GV), head_block(COL_GZ),
            head_block(COL_MQ), head_block(COL_MK), head_block(COL_MV), head_block(COL_MO),
            head_block(COL_MZ),
            pl.BlockSpec((None, None, t, len(GATE_KINDS) * N_DIR), lambda b, h: (b, h, 0, 0)),
            conv_block(0), conv_block(HEADS), conv_block(2 * HEADS),
            conv_block(0), conv_block(HEADS),
            pl.BlockSpec((1, HEAD_DIM), lambda b, h: (0, 0)),
            pl.BlockSpec((1, HEAD_DIM), lambda b, h: (0, h)),
        ],
        out_specs=[out_spec, out_spec],
        out_shape=[out_shape, out_shape],
        scratch_shapes=[
            pltpu.VMEM((t + 2 * CONV_HALO, HEAD_DIM), F32),
            seq(BF16), seq(BF16), seq(BF16), seq(F32),
            seq(BF16), seq(BF16), seq(F32),
            pltpu.VMEM((slots * AQ_ROWS, HEAD_DIM), BF16),
            pltpu.VMEM((slots * HEAD_DIM, HEAD_DIM), F32),
            pltpu.VMEM((slots * CHUNK, HEAD_DIM), F32),
            pltpu.VMEM((slots * 8, LANES), F32),
            pltpu.VMEM((HEAD_DIM, HEAD_DIM), F32),
            pltpu.VMEM((HEAD_DIM, HEAD_DIM), F32),
        ],
        compiler_params=pltpu.CompilerParams(
            dimension_semantics=("parallel", "parallel"), vmem_limit_bytes=VMEM_LIMIT),
        name="mixers",
    )(proj_big, proj_big, proj_big, proj_big, proj_big, proj_big, proj_big, proj_big, proj_big, gsm,
      conv_gdn, conv_gdn, conv_gdn, conv_mlstm, conv_mlstm, gdn_norm_w, mlstm_norm_w)


def _out_proj_kernel(ya_ref, yb_ref, ga_ref, gb_ref, x_ref, wa_ref, wb_ref, wo_ref, gbias_ref, fw_ref,
                     o_ref):
    d = x_ref.shape[1]
    gbias = gbias_ref[...]
    a = jnp.dot(ya_ref[...], wa_ref[...].astype(BF16), preferred_element_type=F32)
    b = jnp.dot(yb_ref[...], wb_ref[...].astype(BF16), preferred_element_type=F32)
    merged = (jax.nn.sigmoid(ga_ref[...].astype(F32) + gbias[:, :d]) * a
              + jax.nn.sigmoid(gb_ref[...].astype(F32) + gbias[:, d:]) * b)
    xo = x_ref[...] + jnp.dot(merged.astype(BF16), wo_ref[...].astype(BF16), preferred_element_type=F32)
    o_ref[...] = xo * lax.rsqrt(jnp.mean(xo * xo, axis=-1, keepdims=True) + NORM_EPS) * fw_ref[...]


def _out_proj(ya, yb, proj_big, x2, wa, wb, wo, gate_bias, final_w):
    m, d = x2.shape
    c = ya.shape[1]
    tm = min(256, m)
    const = lambda shape: pl.BlockSpec(shape, lambda i: (0, 0))
    single = lambda shape: pl.BlockSpec(shape, lambda i: (0, 0), pipeline_mode=pl.Buffered(1))
    return pl.pallas_call(
        _out_proj_kernel,
        grid=(m // tm,),
        in_specs=[
            pl.BlockSpec((tm, c), lambda i: (i, 0)),
            pl.BlockSpec((tm, c), lambda i: (i, 0)),
            pl.BlockSpec((tm, d), lambda i: (i, 0)),
            pl.BlockSpec((tm, d), lambda i: (i, 1)),
            pl.BlockSpec((tm, d), lambda i: (i, 0)),
            single((c, d)), single((c, d)), single((d, d)),
            const((1, 2 * d)), const((1, d)),
        ],
        out_specs=pl.BlockSpec((tm, d), lambda i: (i, 0)),
        out_shape=jax.ShapeDtypeStruct((m, d), F32),
        compiler_params=pltpu.CompilerParams(
            dimension_semantics=("parallel",), vmem_limit_bytes=VMEM_LIMIT),
        name="out_proj",
    )(ya, yb, proj_big, proj_big, x2, wa, wb, wo, gate_bias, final_w)


def kernel(x, w_in, conv_gdn, gdn_a_log, gdn_dt_bias, gdn_norm_w, conv_mlstm, mlstm_i_bias, mlstm_f_bias,
           mlstm_norm_w, gate_bias, w_branch_gdn, w_branch_mlstm, w_out, norm_w, final_norm_w):
    batch, t, d = x.shape
    depth = w_in.shape[0]
    assert depth == 1, "the final rmsnorm is fused into the single layer's output projection"
    key = HEADS * HEAD_DIM
    sizes = (3 * key, key, N_DIR * HEADS, N_DIR * HEADS, 2 * key, key, key, key, N_DIR * HEADS, N_DIR * HEADS,
             2 * d)
    offs = [0]
    for s in sizes:
        offs.append(offs[-1] + s)
    seg = lambda w, i: w[offs[i]:offs[i + 1], :]

    x2 = x.reshape(batch * t, d)
    for layer in range(depth):
        wt = jnp.swapaxes(w_in[layer], 0, 1)
        head_major = lambda a: a.reshape((len(GATE_KINDS), N_DIR, HEADS) + a.shape[1:]).transpose(
            (2, 0, 1) + tuple(range(3, a.ndim + 2))).reshape(a.shape)
        wt_small = head_major(jnp.concatenate([seg(wt, 2), seg(wt, 3), seg(wt, 8), seg(wt, 9)], axis=0))
        wt_small = jnp.pad(wt_small, ((0, LANES - wt_small.shape[0]), (0, 0)))
        row_runs = ((0, offs[10]), (2 * d, offs[0]), (2 * d + 4 * key, offs[4]))
        width = N_DIR * HEADS
        zeros = jnp.zeros((width,), F32)
        gate_params = jnp.stack([
            head_major(jnp.concatenate([zeros, gdn_a_log[layer].reshape(-1), zeros, zeros])),
            head_major(jnp.concatenate([zeros, gdn_dt_bias[layer].reshape(-1),
                                        mlstm_i_bias[layer].reshape(-1), mlstm_f_bias[layer].reshape(-1)]))])
        gate_params = jnp.pad(gate_params, ((0, 0), (0, LANES - 4 * width)))
        proj_big, gates = _in_proj(x2, norm_w[layer][None, :], wt, wt_small, gate_params, row_runs, t)
        ya, yb = _mixers(proj_big, gates, conv_gdn[layer], conv_mlstm[layer], gdn_norm_w[layer][None, :],
                         mlstm_norm_w[layer][None, :], batch, t)
        x2 = _out_proj(ya, yb, proj_big, x2, w_branch_gdn[layer], w_branch_mlstm[layer], w_out[layer],
                       gate_bias[layer][None, :], final_norm_w[None, :])
    return x2.reshape(batch, t, d)
```

```python
import jax
import jax.numpy as jnp
from jax import lax
from jax.experimental import pallas as pl
from jax.experimental.pallas import tpu as pltpu

F32 = jnp.float32
BF16 = jnp.bfloat16

HEADS = 8
HEAD_DIM = 128
CHUNK = 64
CONV_WIDTH = 5
N_DIR = 2
NORM_EPS = 1e-6
LANES = 128
BF16_SUBLANES = 16
CONV_ROWS = 1024
CONV_HALO = 8
CHUNK_GROUP = 8
AQ_ROWS = HEAD_DIM + CHUNK
VMEM_LIMIT = 56 * 1024 * 1024

GATE_KINDS = GATE_BETA, GATE_G, GATE_I, GATE_F = tuple(range(4))

COL_GQ, COL_GK, COL_GV, COL_GZ = 32, 40, 48, 56
COL_MQ, COL_MK, COL_MV, COL_MO, COL_MZ = 64, 72, 80, 88, 96
BIG_COLS = 104 * LANES

_NT_DIMS = (((1,), (1,)), ((), ()))


def _mm(a, b):
    return jnp.dot(a.astype(BF16), b.astype(BF16), preferred_element_type=F32)


def _mm_nt(a, b):
    return lax.dot_general(a.astype(BF16), b.astype(BF16), _NT_DIMS, preferred_element_type=F32)


def _silu(x):
    return x * jax.nn.sigmoid(x)


def _in_proj_kernel(x_ref, nw_ref, wbig_ref, wsmall_ref, gpar_ref, big_ref, gates_ref, n_scr):
    @pl.when(pl.program_id(1) == 0)
    def _():
        x = x_ref[...]
        y = x * lax.rsqrt(jnp.mean(x * x, axis=-1, keepdims=True) + NORM_EPS) * nw_ref[...]
        n_scr[...] = y.astype(BF16)
        pre = lax.dot_general(n_scr[...], wsmall_ref[...].astype(BF16), _NT_DIMS, preferred_element_type=F32)
        z = pre + gpar_ref[1:2, :]
        col = lax.broadcasted_iota(jnp.int32, z.shape, 1)
        per_head = len(GATE_KINDS) * N_DIR
        kind = lax.shift_right_logical(jnp.bitwise_and(col, per_head - 1), 1)
        log_decay = -jnp.exp(gpar_ref[0:1, :]) * jax.nn.softplus(z)
        act = jnp.where(kind == GATE_BETA, jax.nn.sigmoid(z),
                        jnp.where(kind == GATE_G, log_decay,
                                  jnp.where(kind == GATE_I, z, jax.nn.log_sigmoid(z))))
        for h in range(HEADS):
            gates_ref[h] = act[:, h * per_head:(h + 1) * per_head]

    big_ref[...] = lax.dot_general(n_scr[...], wbig_ref[...].astype(BF16), _NT_DIMS,
                                   preferred_element_type=F32).astype(big_ref.dtype)


def _in_proj(x2, norm_w, wt, wt_small, gate_params, row_runs, t):
    m, d = x2.shape
    tm = min(1024, t)
    tn = 1024
    per_head = len(GATE_KINDS) * N_DIR
    blocks_per_seq = t // tm
    assert N_DIR == 2 and t % tm == 0
    assert all(col % tn == 0 and row % BF16_SUBLANES == 0 for col, row in row_runs)

    def source_row(j):
        row = jnp.int32(0)
        for col, first_row in row_runs:
            row = jnp.where(j >= col // tn, first_row + (j - col // tn) * tn, row)
        return pl.multiple_of(row, BF16_SUBLANES)

    return pl.pallas_call(
        _in_proj_kernel,
        grid=(m // tm, BIG_COLS // tn),
        in_specs=[
            pl.BlockSpec((tm, d), lambda i, j: (i, 0)),
            pl.BlockSpec((1, d), lambda i, j: (0, 0)),
            pl.BlockSpec((pl.Element(tn), pl.Element(d)), lambda i, j: (source_row(j), 0)),
            pl.BlockSpec((LANES, d), lambda i, j: (0, 0)),
            pl.BlockSpec((2, LANES), lambda i, j: (0, 0)),
        ],
        out_specs=[
            pl.BlockSpec((tm, tn), lambda i, j: (i, j)),
            pl.BlockSpec((None, HEADS, tm, per_head),
                         lambda i, j: (i // blocks_per_seq, 0, i % blocks_per_seq, 0)),
        ],
        out_shape=[
            jax.ShapeDtypeStruct((m, BIG_COLS), BF16),
            jax.ShapeDtypeStruct((m // t, HEADS, t, per_head), F32),
        ],
        scratch_shapes=[pltpu.VMEM((tm, d), BF16)],
        compiler_params=pltpu.CompilerParams(
            dimension_semantics=("parallel", "arbitrary"), vmem_limit_bytes=VMEM_LIMIT),
        name="in_proj",
    )(x2, norm_w, wt, wt_small, gate_params)


def _conv_silu_into(src_ref, w_ref, pad_scr, dst_scr, post):
    t = src_ref.shape[0]
    rb = min(CONV_ROWS, t)
    zeros = jnp.zeros((CONV_HALO, HEAD_DIM), F32)
    pad_scr[0:CONV_HALO, :] = zeros
    pad_scr[t + CONV_HALO:t + 2 * CONV_HALO, :] = zeros

    def copy(r, carry):
        off = pl.multiple_of(r * rb, rb)
        pad_scr[pl.ds(off + CONV_HALO, rb), :] = src_ref[pl.ds(off, rb), :].astype(F32)
        return carry

    lax.fori_loop(0, t // rb, copy, 0)
    w = w_ref[...]
    pad = (CONV_WIDTH - 1) // 2

    def body(r, carry):
        off = pl.multiple_of(r * rb, rb)
        acc = None
        for j in range(CONV_WIDTH):
            term = pad_scr[pl.ds(off + (CONV_HALO - pad + j), rb), :] * w[j:j + 1, :]
            acc = term if acc is None else acc + term
        dst_scr[pl.ds(off, rb), :] = post(_silu(acc)).astype(dst_scr.dtype)
        return carry

    lax.fori_loop(0, t // rb, body, 0)


def _direction_masks():
    row = lax.broadcasted_iota(jnp.int32, (CHUNK, CHUNK), 0)
    col = lax.broadcasted_iota(jnp.int32, (CHUNK, CHUNK), 1)
    fwd = (row >= col, row > col)
    bwd = (row <= col, row < col)
    out = []
    for (incl, strict), last in ((fwd, CHUNK - 1), (bwd, 0)):
        tri = incl.astype(BF16)
        out.append((incl, strict, jnp.concatenate([tri, tri, tri], axis=1), last))
    return out


def _cumsum_pieces(gates, kind, d):
    c = N_DIR * kind + d
    x = jnp.broadcast_to(gates[:, c:c + 1], (CHUNK, LANES))
    hi = x.astype(BF16)
    rest = x - hi.astype(F32)
    mid = rest.astype(BF16)
    lo = (rest - mid.astype(F32)).astype(BF16)
    return jnp.concatenate([hi, mid, lo], axis=0)


def _as_rows(col_bcast):
    return col_bcast.T[:CHUNK, :]


def _each(fn, *columns):
    return [fn(*args) for args in zip(*columns)]


def _interleave(*generators):
    live = list(generators)
    while live:
        for gen in list(live):
            try:
                next(gen)
            except StopIteration:
                live.remove(gen)


def _pair_masks():
    row = lax.broadcasted_iota(jnp.int32, (CHUNK, CHUNK), 0)
    col = lax.broadcasted_iota(jnp.int32, (CHUNK, CHUNK), 1)
    masks = []
    log_size = 0
    while (1 << log_size) < CHUNK:
        block = lambda x, bits: lax.shift_right_logical(x, bits)
        same_pair = block(row, log_size + 1) == block(col, log_size + 1)
        masks.append(jnp.logical_and(same_pair, block(row, log_size) != block(col, log_size)))
        log_size += 1
    return masks


def _unit_triangular_inverses(lms, eye, pair_masks):
    inv = [eye - jnp.where(pair_masks[0], lm, 0.0) for lm in lms]
    for mask in pair_masks[1:]:
        joined = _each(lambda t, lm: _mm(t, jnp.where(mask, lm, 0.0)), inv, lms)
        yield
        inv = _each(lambda t, tn: t - _mm(tn, t), inv, joined)
        yield
    return inv


def _mixers_kernel(gq_ref, gk_ref, gv_ref, gz_ref, mq_ref, mk_ref, mv_ref, mo_ref, mz_ref, g_ref,
                   cgq_ref, cgk_ref, cgv_ref, cmq_ref, cmk_ref, gnw_ref, mnw_ref,
                   ya_ref, yb_ref,
                   pad_scr, gq_scr, gk_scr, gv_scr, go_scr, mq_scr, mk_scr, mo_scr,
                   aq_pend, b_pend, o_pend, gam_pend, cf_scr, cb_scr):
    t = gq_ref.shape[0]
    n_chunks = t // CHUNK
    group = min(CHUNK_GROUP, n_chunks)
    n_groups = n_chunks // group

    def l2(y):
        return y * lax.rsqrt(jnp.sum(y * y, axis=-1, keepdims=True) + NORM_EPS)

    _conv_silu_into(gq_ref, cgq_ref, pad_scr, gq_scr, lambda y: l2(y) * (HEAD_DIM ** -0.5))
    _conv_silu_into(gk_ref, cgk_ref, pad_scr, gk_scr, l2)
    _conv_silu_into(gv_ref, cgv_ref, pad_scr, gv_scr, lambda y: y)
    _conv_silu_into(mq_ref, cmq_ref, pad_scr, mq_scr, lambda y: y)
    _conv_silu_into(mk_ref, cmk_ref, pad_scr, mk_scr, lambda y: y * (HEAD_DIM ** -0.5))

    go_scr[...] = jnp.zeros_like(go_scr)
    mo_scr[...] = jnp.zeros_like(mo_scr)
    cf_scr[...] = jnp.zeros_like(cf_scr)
    cb_scr[...] = jnp.zeros_like(cb_scr)

    masks = _direction_masks()
    pair_masks = _pair_masks()
    eye = (lax.broadcasted_iota(jnp.int32, (CHUNK, CHUNK), 0)
           == lax.broadcasted_iota(jnp.int32, (CHUNK, CHUNK), 1)).astype(F32)
    ones_col = (lax.broadcasted_iota(jnp.int32, (CHUNK, LANES), 1) == 0).astype(BF16)

    def chains_of(i):
        chains = []
        for j in range(group):
            chains += [(i * group + j, 0), (n_chunks - 1 - (i * group + j), 1)]
        return chains

    def chunk_cumsum(gates, kind, d):
        return jnp.dot(masks[d][2], _cumsum_pieces(gates, kind, d), preferred_element_type=F32)


    def gdn_terms(i):
        chains = chains_of(i)
        dirs = [d for _, d in chains]
        offs = [pl.multiple_of(c * CHUNK, CHUNK) for c, _ in chains]
        q_bf = [gq_scr[pl.ds(off, CHUNK), :] for off in offs]
        k_bf = [gk_scr[pl.ds(off, CHUNK), :] for off in offs]
        kq = _each(lambda k, q: _mm_nt(jnp.concatenate([k, q], axis=0), k), k_bf, q_bf)
        yield
        gt = [g_ref[pl.ds(off, CHUNK), :] for off in offs]
        gcb = _each(lambda x, d: chunk_cumsum(x, GATE_G, d), gt, dirs)
        yield
        qc = [x.astype(F32) for x in q_bf]
        kc = [x.astype(F32) for x in k_bf]
        vc = [gv_scr[pl.ds(off, CHUNK), :].astype(F32) for off in offs]
        beta = _each(lambda x, d: x[:, N_DIR * GATE_BETA + d:N_DIR * GATE_BETA + d + 1], gt, dirs)
        gc_rows = _each(_as_rows, gcb)
        decay = _each(lambda g, r, d: jnp.exp(jnp.where(masks[d][0], g[:, :CHUNK] - r, -jnp.inf)),
                      gcb, gc_rows, dirs)
        lm = _each(lambda x, b, dec, d: jnp.where(masks[d][1], x[:CHUNK] * b * dec, 0.0), kq, beta, decay, dirs)
        attn = _each(lambda x, dec: x[CHUNK:] * dec, kq, decay)
        egc = _each(jnp.exp, gcb)
        rhs = _each(lambda v, k, b, e: jnp.concatenate([v * b, k * b * e], axis=1), vc, kc, beta, egc)
        inv = yield from _unit_triangular_inverses(lm, eye, pair_masks)
        sol = _each(_mm, inv, rhs)
        yield
        g_last = _each(lambda g, d: g[masks[d][3]:masks[d][3] + 1, :], gcb, dirs)
        k_dec_t = _each(lambda k, gl, g: (k * jnp.exp(gl - g)).T, kc, g_last, gcb)
        prod = _each(lambda a, kt, s: _mm(jnp.concatenate([a, kt], axis=0), s), attn, k_dec_t, sol)
        yield
        for slot in range(len(chains)):
            p = prod[slot]
            q_eff = qc[slot] * egc[slot] - p[:CHUNK, HEAD_DIM:]
            aq_pend[slot * AQ_ROWS:(slot + 1) * AQ_ROWS, :] = jnp.concatenate(
                [p[CHUNK:, HEAD_DIM:], q_eff], axis=0).astype(BF16)
            b_pend[slot * HEAD_DIM:(slot + 1) * HEAD_DIM, :] = p[CHUNK:, :HEAD_DIM]
            o_pend[slot * CHUNK:(slot + 1) * CHUNK, :] = p[:CHUNK, :HEAD_DIM]
            gam_pend[slot * 8:(slot + 1) * 8, :] = jnp.broadcast_to(jnp.exp(g_last[slot]), (8, LANES))

    def gdn_advance(state, slot, chunk):
        off = pl.multiple_of(chunk * CHUNK, CHUNK)
        r = jnp.dot(aq_pend[slot * AQ_ROWS:(slot + 1) * AQ_ROWS, :], state.astype(BF16),
                    preferred_element_type=F32)
        go_scr[pl.ds(off, CHUNK), :] += o_pend[slot * CHUNK:(slot + 1) * CHUNK, :] + r[HEAD_DIM:]
        return (state * gam_pend[slot * 8:slot * 8 + 1, :] - r[:HEAD_DIM]
                + b_pend[slot * HEAD_DIM:(slot + 1) * HEAD_DIM, :])

    def gdn_walk(i, states):
        for j in range(group):
            states[0] = gdn_advance(states[0], 2 * j, i * group + j)
            states[1] = gdn_advance(states[1], 2 * j + 1, n_chunks - 1 - (i * group + j))
            yield


    def mlstm_advance(state, terms):
        c_st, n_st, m_st = state
        qc, bcb, m_intra, b_last, m_loc, delta_c, delta_n, intra, off = terms
        m_inter = bcb + m_st
        m_t = jnp.maximum(m_inter, m_intra)
        w_inter = jnp.exp(m_inter - m_t)
        w_intra = jnp.exp(m_intra - m_t)
        inter = _mm(qc, c_st)
        num = w_inter * inter + w_intra * intra[:, :HEAD_DIM]
        den = (w_inter[:, :1] * jnp.sum(qc.astype(F32) * n_st, axis=1, keepdims=True)
               + w_intra[:, :1] * intra[:, HEAD_DIM:HEAD_DIM + 1])
        mo_scr[pl.ds(off, CHUNK), :] += num / jnp.maximum(jnp.abs(den), jnp.exp(-m_t))
        m_new = jnp.maximum(b_last + m_st, m_loc)
        s_old = jnp.exp(b_last + m_st - m_new)
        s_loc = jnp.exp(m_loc - m_new)
        return c_st * s_old + delta_c * s_loc, n_st * s_old + delta_n * s_loc, m_new

    def mlstm_group(i, rows):
        chains = chains_of(i)
        dirs = [d for _, d in chains]
        offs = [pl.multiple_of(c * CHUNK, CHUNK) for c, _ in chains]
        qc = [mq_scr[pl.ds(off, CHUNK), :] for off in offs]
        kc = [mk_scr[pl.ds(off, CHUNK), :] for off in offs]
        v_aug = [jnp.concatenate([mv_ref[pl.ds(off, CHUNK), :], ones_col], axis=1) for off in offs]
        gt = [g_ref[pl.ds(off, CHUNK), :] for off in offs]
        qk = _each(_mm_nt, qc, kc)
        yield
        igb = _each(lambda x, d: jnp.broadcast_to(
            x[:, N_DIR * GATE_I + d:N_DIR * GATE_I + d + 1], (CHUNK, LANES)), gt, dirs)
        bcb = _each(lambda x, d: chunk_cumsum(x, GATE_F, d), gt, dirs)
        yield
        bc_rows = _each(_as_rows, bcb)
        ig_rows = _each(_as_rows, igb)
        d_log = _each(lambda b, br, ir, d: jnp.where(masks[d][0], b[:, :CHUNK] - br + ir, -jnp.inf),
                      bcb, bc_rows, ig_rows, dirs)
        m_intra = _each(lambda x: jnp.max(x, axis=1, keepdims=True), d_log)
        b_last = _each(lambda b, d: b[masks[d][3]:masks[d][3] + 1, :], bcb, dirs)
        a_end = _each(lambda bl, b, ig: bl - b + ig, b_last, bcb, igb)
        m_loc = _each(lambda x: jnp.max(x, axis=0, keepdims=True), a_end)
        k_end = _each(lambda k, a, m: k.astype(F32) * jnp.exp(a - m), kc, a_end, m_loc)
        delta_n = _each(lambda x: jnp.sum(x, axis=0, keepdims=True), k_end)
        delta_c = _each(lambda x, v: _mm(x.T, v[:, :HEAD_DIM]), k_end, v_aug)
        yield
        intra = _each(lambda s, dl, m, v: _mm(s * jnp.exp(dl - m), v), qk, d_log, m_intra, v_aug)
        yield
        terms = [(qc[k], bcb[k], m_intra[k], b_last[k], m_loc[k], delta_c[k], delta_n[k], intra[k], offs[k])
                 for k in range(len(chains))]
        s_fwd = (cf_scr[...], rows[0], rows[1])
        s_bwd = (cb_scr[...], rows[2], rows[3])
        for j in range(group):
            s_fwd = mlstm_advance(s_fwd, terms[2 * j])
            s_bwd = mlstm_advance(s_bwd, terms[2 * j + 1])
            yield
        cf_scr[...] = s_fwd[0]
        cb_scr[...] = s_bwd[0]
        rows[:] = [s_fwd[1], s_fwd[2], s_bwd[1], s_bwd[2]]


    _interleave(gdn_terms(0))

    def step(i, carry):
        states = list(carry[:2])
        rows = list(carry[2:])
        _interleave(gdn_terms(i), gdn_walk(i - 1, states), mlstm_group(i - 1, rows))
        return tuple(states) + tuple(rows)

    zero_state = jnp.zeros((HEAD_DIM, HEAD_DIM), F32)
    zero_row = jnp.zeros((1, LANES), F32)
    carry = lax.fori_loop(1, n_groups, step, (zero_state, zero_state) + (zero_row,) * 4)
    states = list(carry[:2])
    rows = list(carry[2:])
    _interleave(gdn_walk(n_groups - 1, states), mlstm_group(n_groups - 1, rows))


    rb = min(CONV_ROWS, t)
    gnw = gnw_ref[...]
    mnw = mnw_ref[...]

    def epilogue(r, carry):
        off = pl.multiple_of(r * rb, rb)
        o = go_scr[pl.ds(off, rb), :]
        o = o * lax.rsqrt(jnp.mean(o * o, axis=-1, keepdims=True) + NORM_EPS) * gnw
        ya_ref[pl.ds(off, rb), :] = (o * _silu(gz_ref[pl.ds(off, rb), :].astype(F32))).astype(ya_ref.dtype)
        hh = mo_scr[pl.ds(off, rb), :]
        hc = hh - jnp.mean(hh, axis=-1, keepdims=True)
        hn = hc * lax.rsqrt(jnp.mean(hc * hc, axis=-1, keepdims=True) + NORM_EPS) * mnw
        gate = (jax.nn.sigmoid(mo_ref[pl.ds(off, rb), :].astype(F32))
                * _silu(mz_ref[pl.ds(off, rb), :].astype(F32)))
        yb_ref[pl.ds(off, rb), :] = (hn * gate).astype(yb_ref.dtype)
        return carry

    lax.fori_loop(0, t // rb, epilogue, 0)


def _mixers(proj_big, gsm, conv_gdn, conv_mlstm, gdn_norm_w, mlstm_norm_w, batch, t):
    slots = N_DIR * min(CHUNK_GROUP, t // CHUNK)
    head_block = lambda base: pl.BlockSpec((t, HEAD_DIM), lambda b, h: (b, base + h))
    conv_block = lambda base: pl.BlockSpec((CONV_WIDTH, HEAD_DIM), lambda b, h: (0, base + h))
    seq = lambda dtype: pltpu.VMEM((t, HEAD_DIM), dtype)
    out_spec = pl.BlockSpec((t, HEAD_DIM), lambda b, h: (b, h))
    out_shape = jax.ShapeDtypeStruct((batch * t, HEADS * HEAD_DIM), BF16)
    return pl.pallas_call(
        _mixers_kernel,
        grid=(batch, HEADS),
        in_specs=[
            head_block(COL_GQ), head_block(COL_GK), head_block(COL_GV), head_block(COL_GZ),
            head_block(COL_MQ), head_block(COL_MK), head_block(COL_MV), head_block(COL_MO),
            head_block(COL_MZ),
            pl.BlockSpec((None, None, t, len(GATE_KINDS) * N_DIR), lambda b, h: (b, h, 0, 0)),
            conv_block(0), conv_block(HEADS), conv_block(2 * HEADS),
            conv_block(0), conv_block(HEADS),
            pl.BlockSpec((1, HEAD_DIM), lambda b, h: (0, 0)),
            pl.BlockSpec((1, HEAD_DIM), lambda b, h: (0, h)),
        ],
        out_specs=[out_spec, out_spec],
        out_shape=[out_shape, out_shape],
        scratch_shapes=[
            pltpu.VMEM((t + 2 * CONV_HALO, HEAD_DIM), F32),
            seq(BF16), seq(BF16), seq(BF16), seq(F32),
            seq(BF16), seq(BF16), seq(F32),
            pltpu.VMEM((slots * AQ_ROWS, HEAD_DIM), BF16),
            pltpu.VMEM((slots * HEAD_DIM, HEAD_DIM), F32),
            pltpu.VMEM((slots * CHUNK, HEAD_DIM), F32),
            pltpu.VMEM((slots * 8, LANES), F32),
            pltpu.VMEM((HEAD_DIM, HEAD_DIM), F32),
            pltpu.VMEM((HEAD_DIM, HEAD_DIM), F32),
        ],
        compiler_params=pltpu.CompilerParams(
            dimension_semantics=("parallel", "parallel"), vmem_limit_bytes=VMEM_LIMIT),
        name="mixers",
    )(proj_big, proj_big, proj_big, proj_big, proj_big, proj_big, proj_big, proj_big, proj_big, gsm,
      conv_gdn, conv_gdn, conv_gdn, conv_mlstm, conv_mlstm, gdn_norm_w, mlstm_norm_w)


def _out_proj_kernel(ya_ref, yb_ref, ga_ref, gb_ref, x_ref, wa_ref, wb_ref, wo_ref, gbias_ref, fw_ref,
                     o_ref):
    d = x_ref.shape[1]
    gbias = gbias_ref[...]
    a = jnp.dot(ya_ref[...], wa_ref[...].astype(BF16), preferred_element_type=F32)
    b = jnp.dot(yb_ref[...], wb_ref[...].astype(BF16), preferred_element_type=F32)
    merged = (jax.nn.sigmoid(ga_ref[...].astype(F32) + gbias[:, :d]) * a
              + jax.nn.sigmoid(gb_ref[...].astype(F32) + gbias[:, d:]) * b)
    xo = x_ref[...] + jnp.dot(merged.astype(BF16), wo_ref[...].astype(BF16), preferred_element_type=F32)
    o_ref[...] = xo * lax.rsqrt(jnp.mean(xo * xo, axis=-1, keepdims=True) + NORM_EPS) * fw_ref[...]


def _out_proj(ya, yb, proj_big, x2, wa, wb, wo, gate_bias, final_w):
    m, d = x2.shape
    c = ya.shape[1]
    tm = min(256, m)
    const = lambda shape: pl.BlockSpec(shape, lambda i: (0, 0))
    single = lambda shape: pl.BlockSpec(shape, lambda i: (0, 0), pipeline_mode=pl.Buffered(1))
    return pl.pallas_call(
        _out_proj_kernel,
        grid=(m // tm,),
        in_specs=[
            pl.BlockSpec((tm, c), lambda i: (i, 0)),
            pl.BlockSpec((tm, c), lambda i: (i, 0)),
            pl.BlockSpec((tm, d), lambda i: (i, 0)),
            pl.BlockSpec((tm, d), lambda i: (i, 1)),
            pl.BlockSpec((tm, d), lambda i: (i, 0)),
            single((c, d)), single((c, d)), single((d, d)),
            const((1, 2 * d)), const((1, d)),
        ],
        out_specs=pl.BlockSpec((tm, d), lambda i: (i, 0)),
        out_shape=jax.ShapeDtypeStruct((m, d), F32),
        compiler_params=pltpu.CompilerParams(
            dimension_semantics=("parallel",), vmem_limit_bytes=VMEM_LIMIT),
        name="out_proj",
    )(ya, yb, proj_big, proj_big, x2, wa, wb, wo, gate_bias, final_w)


def kernel(x, w_in, conv_gdn, gdn_a_log, gdn_dt_bias, gdn_norm_w, conv_mlstm, mlstm_i_bias, mlstm_f_bias,
           mlstm_norm_w, gate_bias, w_branch_gdn, w_branch_mlstm, w_out, norm_w, final_norm_w):
    batch, t, d = x.shape
    depth = w_in.shape[0]
    assert depth == 1, "the final rmsnorm is fused into the single layer's output projection"
    key = HEADS * HEAD_DIM
    sizes = (3 * key, key, N_DIR * HEADS, N_DIR * HEADS, 2 * key, key, key, key, N_DIR * HEADS, N_DIR * HEADS,
             2 * d)
    offs = [0]
    for s in sizes:
        offs.append(offs[-1] + s)
    seg = lambda w, i: w[offs[i]:offs[i + 1], :]

    x2 = x.reshape(batch * t, d)
    for layer in range(depth):
        wt = jnp.swapaxes(w_in[layer], 0, 1)
        head_major = lambda a: a.reshape((len(GATE_KINDS), N_DIR, HEADS) + a.shape[1:]).transpose(
            (2, 0, 1) + tuple(range(3, a.ndim + 2))).reshape(a.shape)
        wt_small = head_major(jnp.concatenate([seg(wt, 2), seg(wt, 3), seg(wt, 8), seg(wt, 9)], axis=0))
        wt_small = jnp.pad(wt_small, ((0, LANES - wt_small.shape[0]), (0, 0)))
        row_runs = ((0, offs[10]), (2 * d, offs[0]), (2 * d + 4 * key, offs[4]))
        width = N_DIR * HEADS
        zeros = jnp.zeros((width,), F32)
        gate_params = jnp.stack([
            head_major(jnp.concatenate([zeros, gdn_a_log[layer].reshape(-1), zeros, zeros])),
            head_major(jnp.concatenate([zeros, gdn_dt_bias[layer].reshape(-1),
                                        mlstm_i_bias[layer].reshape(-1), mlstm_f_bias[layer].reshape(-1)]))])
        gate_params = jnp.pad(gate_params, ((0, 0), (0, LANES - 4 * width)))
        proj_big, gates = _in_proj(x2, norm_w[layer][None, :], wt, wt_small, gate_params, row_runs, t)
        ya, yb = _mixers(proj_big, gates, conv_gdn[layer], conv_mlstm[layer], gdn_norm_w[layer][None, :],
                         mlstm_norm_w[layer][None, :], batch, t)
        x2 = _out_proj(ya, yb, proj_big, x2, w_branch_gdn[layer], w_branch_mlstm[layer], w_out[layer],
                       gate_bias[layer][None, :], final_norm_w[None, :])
    return x2.reshape(batch, t, d)
```

```python
import jax
import jax.numpy as jnp
from jax import lax
from jax.experimental import pallas as pl
from jax.experimental.pallas import tpu as pltpu

F32 = jnp.float32
BF16 = jnp.bfloat16

HEADS = 8
HEAD_DIM = 128
CHUNK = 64
CONV_WIDTH = 5
N_DIR = 2
NORM_EPS = 1e-6
LANES = 128
BF16_SUBLANES = 16
CONV_ROWS = 1024
CONV_HALO = 8
CHUNK_GROUP = 8
AQ_ROWS = HEAD_DIM + CHUNK
VMEM_LIMIT = 56 * 1024 * 1024

GATE_KINDS = GATE_BETA, GATE_G, GATE_I, GATE_F = tuple(range(4))

COL_GQ, COL_GK, COL_GV, COL_GZ = 32, 40, 48, 56
COL_MQ, COL_MK, COL_MV, COL_MO, COL_MZ = 64, 72, 80, 88, 96
BIG_COLS = 104 * LANES

_NT_DIMS = (((1,), (1,)), ((), ()))


def _mm(a, b):
    return jnp.dot(a.astype(BF16), b.astype(BF16), preferred_element_type=F32)


def _mm_nt(a, b):
    return lax.dot_general(a.astype(BF16), b.astype(BF16), _NT_DIMS, preferred_element_type=F32)


def _sigmoid(x):
    return 0.5 * jnp.tanh(0.5 * x) + 0.5


def _silu(x):
    h = 0.5 * x
    return h * jnp.tanh(h) + h


def _in_proj_kernel(x_ref, nw_ref, wbig_ref, wsmall_ref, gpar_ref, big_ref, gates_ref, n_scr):
    @pl.when(pl.program_id(1) == 0)
    def _():
        x = x_ref[...]
        y = x * lax.rsqrt(jnp.mean(x * x, axis=-1, keepdims=True) + NORM_EPS) * nw_ref[...]
        n_scr[...] = y.astype(BF16)
        pre = lax.dot_general(n_scr[...], wsmall_ref[...].astype(BF16), _NT_DIMS, preferred_element_type=F32)
        z = pre + gpar_ref[1:2, :]
        col = lax.broadcasted_iota(jnp.int32, z.shape, 1)
        per_head = len(GATE_KINDS) * N_DIR
        kind = lax.shift_right_logical(jnp.bitwise_and(col, per_head - 1), 1)
        log_decay = -jnp.exp(gpar_ref[0:1, :]) * jax.nn.softplus(z)
        act = jnp.where(kind == GATE_BETA, jax.nn.sigmoid(z),
                        jnp.where(kind == GATE_G, log_decay,
                                  jnp.where(kind == GATE_I, z, jax.nn.log_sigmoid(z))))
        for h in range(HEADS):
            gates_ref[h] = act[:, h * per_head:(h + 1) * per_head]

    big_ref[...] = lax.dot_general(n_scr[...], wbig_ref[...].astype(BF16), _NT_DIMS,
                                   preferred_element_type=F32).astype(big_ref.dtype)


def _in_proj(x2, norm_w, wt, wt_small, gate_params, row_runs, t):
    m, d = x2.shape
    tm = min(1024, t)
    tn = 1024
    per_head = len(GATE_KINDS) * N_DIR
    blocks_per_seq = t // tm
    assert N_DIR == 2 and t % tm == 0
    assert all(col % tn == 0 and row % BF16_SUBLANES == 0 for col, row in row_runs)

    def source_row(j):
        row = jnp.int32(0)
        for col, first_row in row_runs:
            row = jnp.where(j >= col // tn, first_row + (j - col // tn) * tn, row)
        return pl.multiple_of(row, BF16_SUBLANES)

    return pl.pallas_call(
        _in_proj_kernel,
        grid=(m // tm, BIG_COLS // tn),
        in_specs=[
            pl.BlockSpec((tm, d), lambda i, j: (i, 0)),
            pl.BlockSpec((1, d), lambda i, j: (0, 0)),
            pl.BlockSpec((pl.Element(tn), pl.Element(d)), lambda i, j: (source_row(j), 0)),
            pl.BlockSpec((LANES, d), lambda i, j: (0, 0)),
            pl.BlockSpec((2, LANES), lambda i, j: (0, 0)),
        ],
        out_specs=[
            pl.BlockSpec((tm, tn), lambda i, j: (i, j)),
            pl.BlockSpec((None, HEADS, tm, per_head),
                         lambda i, j: (i // blocks_per_seq, 0, i % blocks_per_seq, 0)),
        ],
        out_shape=[
            jax.ShapeDtypeStruct((m, BIG_COLS), BF16),
            jax.ShapeDtypeStruct((m // t, HEADS, t, per_head), F32),
        ],
        scratch_shapes=[pltpu.VMEM((tm, d), BF16)],
        compiler_params=pltpu.CompilerParams(
            dimension_semantics=("parallel", "arbitrary"), vmem_limit_bytes=VMEM_LIMIT),
        name="in_proj",
    )(x2, norm_w, wt, wt_small, gate_params)


def _conv_silu_into(src_ref, w_ref, pad_scr, dst_scr, post):
    t = src_ref.shape[0]
    rb = min(CONV_ROWS, t)
    zeros = jnp.zeros((CONV_HALO, HEAD_DIM), F32)
    pad_scr[0:CONV_HALO, :] = zeros
    pad_scr[t + CONV_HALO:t + 2 * CONV_HALO, :] = zeros

    def copy(r, carry):
        off = pl.multiple_of(r * rb, rb)
        pad_scr[pl.ds(off + CONV_HALO, rb), :] = src_ref[pl.ds(off, rb), :].astype(F32)
        return carry

    lax.fori_loop(0, t // rb, copy, 0)
    w = w_ref[...]
    pad = (CONV_WIDTH - 1) // 2

    def body(r, carry):
        off = pl.multiple_of(r * rb, rb)
        acc = None
        for j in range(CONV_WIDTH):
            term = pad_scr[pl.ds(off + (CONV_HALO - pad + j), rb), :] * w[j:j + 1, :]
            acc = term if acc is None else acc + term
        dst_scr[pl.ds(off, rb), :] = post(_silu(acc)).astype(dst_scr.dtype)
        return carry

    lax.fori_loop(0, t // rb, body, 0)


def _direction_masks():
    row = lax.broadcasted_iota(jnp.int32, (CHUNK, CHUNK), 0)
    col = lax.broadcasted_iota(jnp.int32, (CHUNK, CHUNK), 1)
    fwd = (row >= col, row > col)
    bwd = (row <= col, row < col)
    out = []
    for (incl, strict), last in ((fwd, CHUNK - 1), (bwd, 0)):
        tri = incl.astype(BF16)
        out.append((incl, strict, jnp.concatenate([tri, tri, tri], axis=1), last))
    return out


def _cumsum_pieces(gates, kind, d):
    c = N_DIR * kind + d
    x = jnp.broadcast_to(gates[:, c:c + 1], (CHUNK, LANES))
    hi = x.astype(BF16)
    rest = x - hi.astype(F32)
    mid = rest.astype(BF16)
    lo = (rest - mid.astype(F32)).astype(BF16)
    return jnp.concatenate([hi, mid, lo], axis=0)


def _as_rows(col_bcast):
    return col_bcast.T[:CHUNK, :]


def _each(fn, *columns):
    return [fn(*args) for args in zip(*columns)]


def _interleave(*generators):
    live = list(generators)
    while live:
        for gen in list(live):
            try:
                next(gen)
            except StopIteration:
                live.remove(gen)


def _wide_index():
    row = lax.broadcasted_iota(jnp.int32, (CHUNK, 2 * CHUNK), 0)
    lane = lax.broadcasted_iota(jnp.int32, (CHUNK, 2 * CHUNK), 1)
    return row, jnp.bitwise_and(lane, CHUNK - 1), lane >= CHUNK


def _pair_masks():
    row, col, right = _wide_index()
    masks = []
    log_size = 0
    while (1 << log_size) < CHUNK:
        block = lambda x, bits: lax.shift_right_logical(x, bits)
        same_pair = block(row, log_size + 1) == block(col, log_size + 1)
        halves = block(row, log_size) != block(col, log_size)
        masks.append(jnp.logical_and(right, jnp.logical_and(same_pair, halves)))
        log_size += 1
    return masks


def _unit_triangular_inverses(lms_wide, pair_masks):
    row, col, right = _wide_index()
    identity = jnp.logical_and(jnp.logical_not(right), row == col).astype(F32)
    packed = [jnp.where(right, lm, identity) for lm in lms_wide]
    zero_rows = jnp.zeros((CHUNK, 2 * CHUNK), BF16)
    for mask in pair_masks:
        def level(p):
            joined = jnp.where(mask, p, 0.0).astype(BF16)
            return jnp.dot(joined, jnp.concatenate([zero_rows, p.astype(BF16)], axis=0),
                           preferred_element_type=F32)

        prods = _each(level, packed)
        yield
        packed = _each(lambda p, d: p - d, packed, prods)
    return [p[:, :CHUNK] for p in packed]


def _mixers_kernel(gq_ref, gk_ref, gv_ref, gz_ref, mq_ref, mk_ref, mv_ref, mo_ref, mz_ref, g_ref,
                   cgq_ref, cgk_ref, cgv_ref, cmq_ref, cmk_ref, gnw_ref, mnw_ref,
                   ya_ref, yb_ref,
                   pad_scr, gq_scr, gk_scr, gv_scr, go_scr, mq_scr, mk_scr, mo_scr,
                   aq_pend, b_pend, o_pend, gam_pend, cf_scr, cb_scr):
    t = gq_ref.shape[0]
    n_chunks = t // CHUNK
    group = min(CHUNK_GROUP, n_chunks)
    n_groups = n_chunks // group

    def l2(y, scale_sq=1.0):
        total = jnp.sum(y * y, axis=-1, keepdims=True) + NORM_EPS
        return y * lax.rsqrt(total if scale_sq == 1.0 else total * (1.0 / scale_sq))

    _conv_silu_into(gq_ref, cgq_ref, pad_scr, gq_scr, lambda y: l2(y, 1.0 / HEAD_DIM))
    _conv_silu_into(gk_ref, cgk_ref, pad_scr, gk_scr, l2)
    _conv_silu_into(gv_ref, cgv_ref, pad_scr, gv_scr, lambda y: y)
    _conv_silu_into(mq_ref, cmq_ref, pad_scr, mq_scr, lambda y: y)
    _conv_silu_into(mk_ref, cmk_ref, pad_scr, mk_scr, lambda y: y * (HEAD_DIM ** -0.5))

    go_scr[...] = jnp.zeros_like(go_scr)
    mo_scr[...] = jnp.zeros_like(mo_scr)
    cf_scr[...] = jnp.zeros_like(cf_scr)
    cb_scr[...] = jnp.zeros_like(cb_scr)

    masks = _direction_masks()
    pair_masks = _pair_masks()
    wide_row, wide_col, _ = _wide_index()
    wide_masks = [(wide_row >= wide_col, wide_row > wide_col), (wide_row <= wide_col, wide_row < wide_col)]
    ones_col = (lax.broadcasted_iota(jnp.int32, (CHUNK, LANES), 1) == 0).astype(BF16)

    def chains_of(i):
        chains = []
        for j in range(group):
            chains += [(i * group + j, 0), (n_chunks - 1 - (i * group + j), 1)]
        return chains

    def chunk_cumsum(gates, kind, d):
        return jnp.dot(masks[d][2], _cumsum_pieces(gates, kind, d), preferred_element_type=F32)


    def gdn_terms(i):
        chains = chains_of(i)
        dirs = [d for _, d in chains]
        offs = [pl.multiple_of(c * CHUNK, CHUNK) for c, _ in chains]
        q_bf = [gq_scr[pl.ds(off, CHUNK), :] for off in offs]
        k_bf = [gk_scr[pl.ds(off, CHUNK), :] for off in offs]
        kq = _each(lambda k, q: _mm_nt(jnp.concatenate([k, q], axis=0), jnp.concatenate([k, k], axis=0)),
                   k_bf, q_bf)
        yield
        gt = [g_ref[pl.ds(off, CHUNK), :] for off in offs]
        gcb = _each(lambda x, d: chunk_cumsum(x, GATE_G, d), gt, dirs)
        yield
        qc = [x.astype(F32) for x in q_bf]
        kc = [x.astype(F32) for x in k_bf]
        vc = [gv_scr[pl.ds(off, CHUNK), :].astype(F32) for off in offs]
        beta = _each(lambda x, d: x[:, N_DIR * GATE_BETA + d:N_DIR * GATE_BETA + d + 1], gt, dirs)
        gc_rows = _each(lambda g: jnp.concatenate([g, g], axis=0).T[:CHUNK, :], gcb)
        decay = _each(lambda g, r, d: jnp.exp(jnp.where(wide_masks[d][0], g - r, -jnp.inf)), gcb, gc_rows, dirs)
        lm = _each(lambda x, b, dec, d: jnp.where(wide_masks[d][1], x[:CHUNK] * b * dec, 0.0),
                   kq, beta, decay, dirs)
        attn = _each(lambda x, dec: x[CHUNK:, :CHUNK] * dec[:, :CHUNK], kq, decay)
        egc = _each(jnp.exp, gcb)
        rhs = _each(lambda v, k, b, e: jnp.concatenate([v * b, k * b * e], axis=1), vc, kc, beta, egc)
        inv = yield from _unit_triangular_inverses(lm, pair_masks)
        sol = _each(_mm, inv, rhs)
        yield
        g_last = _each(lambda g, d: g[masks[d][3]:masks[d][3] + 1, :], gcb, dirs)
        k_dec_t = _each(lambda k, gl, g: (k * jnp.exp(gl - g)).T, kc, g_last, gcb)
        prod = _each(lambda a, kt, s: _mm(jnp.concatenate([a, kt], axis=0), s), attn, k_dec_t, sol)
        yield
        for slot in range(len(chains)):
            p = prod[slot]
            q_eff = qc[slot] * egc[slot] - p[:CHUNK, HEAD_DIM:]
            aq_pend[slot * AQ_ROWS:(slot + 1) * AQ_ROWS, :] = jnp.concatenate(
                [p[CHUNK:, HEAD_DIM:], q_eff], axis=0).astype(BF16)
            b_pend[slot * HEAD_DIM:(slot + 1) * HEAD_DIM, :] = p[CHUNK:, :HEAD_DIM]
            o_pend[slot * CHUNK:(slot + 1) * CHUNK, :] = p[:CHUNK, :HEAD_DIM]
            gam_pend[slot * 8:(slot + 1) * 8, :] = jnp.broadcast_to(jnp.exp(g_last[slot]), (8, LANES))

    def gdn_advance(state, slot, chunk):
        off = pl.multiple_of(chunk * CHUNK, CHUNK)
        r = jnp.dot(aq_pend[slot * AQ_ROWS:(slot + 1) * AQ_ROWS, :], state.astype(BF16),
                    preferred_element_type=F32)
        go_scr[pl.ds(off, CHUNK), :] += o_pend[slot * CHUNK:(slot + 1) * CHUNK, :] + r[HEAD_DIM:]
        return (state * gam_pend[slot * 8:slot * 8 + 1, :] - r[:HEAD_DIM]
                + b_pend[slot * HEAD_DIM:(slot + 1) * HEAD_DIM, :])

    def gdn_walk(i, states):
        for j in range(group):
            states[0] = gdn_advance(states[0], 2 * j, i * group + j)
            states[1] = gdn_advance(states[1], 2 * j + 1, n_chunks - 1 - (i * group + j))
            yield


    def mlstm_advance(state, terms):
        c_st, n_st, m_st = state
        qc, bcb, m_intra, b_last, m_loc, delta_c, delta_n, intra, off = terms
        m_inter = bcb + m_st
        m_t = jnp.maximum(m_inter, m_intra)
        w_inter = jnp.exp(m_inter - m_t)
        w_intra = jnp.exp(m_intra - m_t)
        inter = _mm(qc, c_st)
        num = w_inter * inter + w_intra * intra[:, :HEAD_DIM]
        den = (w_inter[:, :1] * jnp.sum(qc.astype(F32) * n_st, axis=1, keepdims=True)
               + w_intra[:, :1] * intra[:, HEAD_DIM:HEAD_DIM + 1])
        mo_scr[pl.ds(off, CHUNK), :] += num / jnp.maximum(jnp.abs(den), jnp.exp(-m_t))
        m_new = jnp.maximum(b_last + m_st, m_loc)
        s_old = jnp.exp(b_last + m_st - m_new)
        s_loc = jnp.exp(m_loc - m_new)
        return c_st * s_old + delta_c * s_loc, n_st * s_old + delta_n * s_loc, m_new

    def mlstm_group(i, rows):
        chains = chains_of(i)
        dirs = [d for _, d in chains]
        offs = [pl.multiple_of(c * CHUNK, CHUNK) for c, _ in chains]
        qc = [mq_scr[pl.ds(off, CHUNK), :] for off in offs]
        kc = [mk_scr[pl.ds(off, CHUNK), :] for off in offs]
        v_aug = [jnp.concatenate([mv_ref[pl.ds(off, CHUNK), :], ones_col], axis=1) for off in offs]
        gt = [g_ref[pl.ds(off, CHUNK), :] for off in offs]
        qk = _each(_mm_nt, qc, kc)
        yield
        igb = _each(lambda x, d: jnp.broadcast_to(
            x[:, N_DIR * GATE_I + d:N_DIR * GATE_I + d + 1], (CHUNK, LANES)), gt, dirs)
        bcb = _each(lambda x, d: chunk_cumsum(x, GATE_F, d), gt, dirs)
        yield
        bc_rows = _each(_as_rows, bcb)
        ig_rows = _each(_as_rows, igb)
        d_log = _each(lambda b, br, ir, d: jnp.where(masks[d][0], b[:, :CHUNK] - br + ir, -jnp.inf),
                      bcb, bc_rows, ig_rows, dirs)
        m_intra = _each(lambda x: jnp.max(x, axis=1, keepdims=True), d_log)
        b_last = _each(lambda b, d: b[masks[d][3]:masks[d][3] + 1, :], bcb, dirs)
        a_end = _each(lambda bl, b, ig: bl - b + ig, b_last, bcb, igb)
        m_loc = _each(lambda x: jnp.max(x, axis=0, keepdims=True), a_end)
        k_end = _each(lambda k, a, m: k.astype(F32) * jnp.exp(a - m), kc, a_end, m_loc)
        delta_n = _each(lambda x: jnp.sum(x, axis=0, keepdims=True), k_end)
        delta_c = _each(lambda x, v: _mm(x.T, v[:, :HEAD_DIM]), k_end, v_aug)
        yield
        intra = _each(lambda s, dl, m, v: _mm(s * jnp.exp(dl - m), v), qk, d_log, m_intra, v_aug)
        yield
        terms = [(qc[k], bcb[k], m_intra[k], b_last[k], m_loc[k], delta_c[k], delta_n[k], intra[k], offs[k])
                 for k in range(len(chains))]
        s_fwd = (cf_scr[...], rows[0], rows[1])
        s_bwd = (cb_scr[...], rows[2], rows[3])
        for j in range(group):
            s_fwd = mlstm_advance(s_fwd, terms[2 * j])
            s_bwd = mlstm_advance(s_bwd, terms[2 * j + 1])
            yield
        cf_scr[...] = s_fwd[0]
        cb_scr[...] = s_bwd[0]
        rows[:] = [s_fwd[1], s_fwd[2], s_bwd[1], s_bwd[2]]


    _interleave(gdn_terms(0))

    def step(i, carry):
        states = list(carry[:2])
        rows = list(carry[2:])
        _interleave(gdn_terms(i), gdn_walk(i - 1, states), mlstm_group(i - 1, rows))
        return tuple(states) + tuple(rows)

    zero_state = jnp.zeros((HEAD_DIM, HEAD_DIM), F32)
    zero_row = jnp.zeros((1, LANES), F32)
    carry = lax.fori_loop(1, n_groups, step, (zero_state, zero_state) + (zero_row,) * 4)
    states = list(carry[:2])
    rows = list(carry[2:])
    _interleave(gdn_walk(n_groups - 1, states), mlstm_group(n_groups - 1, rows))


    rb = min(CONV_ROWS, t)
    gnw = gnw_ref[...]
    mnw = mnw_ref[...]

    def epilogue(r, carry):
        off = pl.multiple_of(r * rb, rb)
        o = go_scr[pl.ds(off, rb), :]
        o = o * lax.rsqrt(jnp.mean(o * o, axis=-1, keepdims=True) + NORM_EPS) * gnw
        ya_ref[pl.ds(off, rb), :] = (o * _silu(gz_ref[pl.ds(off, rb), :].astype(F32))).astype(ya_ref.dtype)
        hh = mo_scr[pl.ds(off, rb), :]
        hc = hh - jnp.mean(hh, axis=-1, keepdims=True)
        hn = hc * lax.rsqrt(jnp.mean(hc * hc, axis=-1, keepdims=True) + NORM_EPS) * mnw
        gate = (_sigmoid(mo_ref[pl.ds(off, rb), :].astype(F32))
                * _silu(mz_ref[pl.ds(off, rb), :].astype(F32)))
        yb_ref[pl.ds(off, rb), :] = (hn * gate).astype(yb_ref.dtype)
        return carry

    lax.fori_loop(0, t // rb, epilogue, 0)


def _mixers(proj_big, gsm, conv_gdn, conv_mlstm, gdn_norm_w, mlstm_norm_w, batch, t):
    slots = N_DIR * min(CHUNK_GROUP, t // CHUNK)
    head_block = lambda base: pl.BlockSpec((t, HEAD_DIM), lambda b, h: (b, base + h))
    conv_block = lambda base: pl.BlockSpec((CONV_WIDTH, HEAD_DIM), lambda b, h: (0, base + h))
    seq = lambda dtype: pltpu.VMEM((t, HEAD_DIM), dtype)
    out_spec = pl.BlockSpec((t, HEAD_DIM), lambda b, h: (b, h))
    out_shape = jax.ShapeDtypeStruct((batch * t, HEADS * HEAD_DIM), BF16)
    return pl.pallas_call(
        _mixers_kernel,
        grid=(batch, HEADS),
        in_specs=[
            head_block(COL_GQ), head_block(COL_GK), head_block(COL_GV), head_block(COL_GZ),
            head_block(COL_MQ), head_block(COL_MK), head_block(COL_MV), head_block(COL_MO),
            head_block(COL_MZ),
            pl.BlockSpec((None, None, t, len(GATE_KINDS) * N_DIR), lambda b, h: (b, h, 0, 0)),
            conv_block(0), conv_block(HEADS), conv_block(2 * HEADS),
            conv_block(0), conv_block(HEADS),
            pl.BlockSpec((1, HEAD_DIM), lambda b, h: (0, 0)),
            pl.BlockSpec((1, HEAD_DIM), lambda b, h: (0, h)),
        ],
        out_specs=[out_spec, out_spec],
        out_shape=[out_shape, out_shape],
        scratch_shapes=[
            pltpu.VMEM((t + 2 * CONV_HALO, HEAD_DIM), F32),
            seq(BF16), seq(BF16), seq(BF16), seq(F32),
            seq(BF16), seq(BF16), seq(F32),
            pltpu.VMEM((slots * AQ_ROWS, HEAD_DIM), BF16),
            pltpu.VMEM((slots * HEAD_DIM, HEAD_DIM), F32),
            pltpu.VMEM((slots * CHUNK, HEAD_DIM), F32),
            pltpu.VMEM((slots * 8, LANES), F32),
            pltpu.VMEM((HEAD_DIM, HEAD_DIM), F32),
            pltpu.VMEM((HEAD_DIM, HEAD_DIM), F32),
        ],
        compiler_params=pltpu.CompilerParams(
            dimension_semantics=("parallel", "parallel"), vmem_limit_bytes=VMEM_LIMIT),
        name="mixers",
    )(proj_big, proj_big, proj_big, proj_big, proj_big, proj_big, proj_big, proj_big, proj_big, gsm,
      conv_gdn, conv_gdn, conv_gdn, conv_mlstm, conv_mlstm, gdn_norm_w, mlstm_norm_w)


def _out_proj_kernel(ya_ref, yb_ref, ga_ref, gb_ref, x_ref, wa_ref, wb_ref, wo_ref, gbias_ref, fw_ref,
                     o_ref):
    d = x_ref.shape[1]
    gbias = gbias_ref[...]
    a = jnp.dot(ya_ref[...], wa_ref[...].astype(BF16), preferred_element_type=F32)
    b = jnp.dot(yb_ref[...], wb_ref[...].astype(BF16), preferred_element_type=F32)
    merged = (jax.nn.sigmoid(ga_ref[...].astype(F32) + gbias[:, :d]) * a
              + jax.nn.sigmoid(gb_ref[...].astype(F32) + gbias[:, d:]) * b)
    xo = x_ref[...] + jnp.dot(merged.astype(BF16), wo_ref[...].astype(BF16), preferred_element_type=F32)
    o_ref[...] = xo * lax.rsqrt(jnp.mean(xo * xo, axis=-1, keepdims=True) + NORM_EPS) * fw_ref[...]


def _out_proj(ya, yb, proj_big, x2, wa, wb, wo, gate_bias, final_w):
    m, d = x2.shape
    c = ya.shape[1]
    tm = min(256, m)
    const = lambda shape: pl.BlockSpec(shape, lambda i: (0, 0))
    single = lambda shape: pl.BlockSpec(shape, lambda i: (0, 0), pipeline_mode=pl.Buffered(1))
    return pl.pallas_call(
        _out_proj_kernel,
        grid=(m // tm,),
        in_specs=[
            pl.BlockSpec((tm, c), lambda i: (i, 0)),
            pl.BlockSpec((tm, c), lambda i: (i, 0)),
            pl.BlockSpec((tm, d), lambda i: (i, 0)),
            pl.BlockSpec((tm, d), lambda i: (i, 1)),
            pl.BlockSpec((tm, d), lambda i: (i, 0)),
            single((c, d)), single((c, d)), single((d, d)),
            const((1, 2 * d)), const((1, d)),
        ],
        out_specs=pl.BlockSpec((tm, d), lambda i: (i, 0)),
        out_shape=jax.ShapeDtypeStruct((m, d), F32),
        compiler_params=pltpu.CompilerParams(
            dimension_semantics=("parallel",), vmem_limit_bytes=VMEM_LIMIT),
        name="out_proj",
    )(ya, yb, proj_big, proj_big, x2, wa, wb, wo, gate_bias, final_w)


def kernel(x, w_in, conv_gdn, gdn_a_log, gdn_dt_bias, gdn_norm_w, conv_mlstm, mlstm_i_bias, mlstm_f_bias,
           mlstm_norm_w, gate_bias, w_branch_gdn, w_branch_mlstm, w_out, norm_w, final_norm_w):
    batch, t, d = x.shape
    depth = w_in.shape[0]
    assert depth == 1, "the final rmsnorm is fused into the single layer's output projection"
    key = HEADS * HEAD_DIM
    sizes = (3 * key, key, N_DIR * HEADS, N_DIR * HEADS, 2 * key, key, key, key, N_DIR * HEADS, N_DIR * HEADS,
             2 * d)
    offs = [0]
    for s in sizes:
        offs.append(offs[-1] + s)
    seg = lambda w, i: w[offs[i]:offs[i + 1], :]

    x2 = x.reshape(batch * t, d)
    for layer in range(depth):
        wt = jnp.swapaxes(w_in[layer], 0, 1)
        head_major = lambda a: a.reshape((len(GATE_KINDS), N_DIR, HEADS) + a.shape[1:]).transpose(
            (2, 0, 1) + tuple(range(3, a.ndim + 2))).reshape(a.shape)
        wt_small = head_major(jnp.concatenate([seg(wt, 2), seg(wt, 3), seg(wt, 8), seg(wt, 9)], axis=0))
        wt_small = jnp.pad(wt_small, ((0, LANES - wt_small.shape[0]), (0, 0)))
        row_runs = ((0, offs[10]), (2 * d, offs[0]), (2 * d + 4 * key, offs[4]))
        width = N_DIR * HEADS
        zeros = jnp.zeros((width,), F32)
        gate_params = jnp.stack([
            head_major(jnp.concatenate([zeros, gdn_a_log[layer].reshape(-1), zeros, zeros])),
            head_major(jnp.concatenate([zeros, gdn_dt_bias[layer].reshape(-1),
                                        mlstm_i_bias[layer].reshape(-1), mlstm_f_bias[layer].reshape(-1)]))])
        gate_params = jnp.pad(gate_params, ((0, 0), (0, LANES - 4 * width)))
        proj_big, gates = _in_proj(x2, norm_w[layer][None, :], wt, wt_small, gate_params, row_runs, t)
        ya, yb = _mixers(proj_big, gates, conv_gdn[layer], conv_mlstm[layer], gdn_norm_w[layer][None, :],
                         mlstm_norm_w[layer][None, :], batch, t)
        x2 = _out_proj(ya, yb, proj_big, x2, w_branch_gdn[layer], w_branch_mlstm[layer], w_out[layer],
                       gate_bias[layer][None, :], final_norm_w[None, :])
    return x2.reshape(batch, t, d)
```

```python
import jax
import jax.numpy as jnp
from jax import lax
from jax.experimental import pallas as pl
from jax.experimental.pallas import tpu as pltpu

F32 = jnp.float32
BF16 = jnp.bfloat16

HEADS = 8
HEAD_DIM = 128
CHUNK = 64
CONV_WIDTH = 5
N_DIR = 2
NORM_EPS = 1e-6
LANES = 128
BF16_SUBLANES = 16
CONV_ROWS = 1024
CONV_HALO = 8
CHUNK_GROUP = 8
AQ_ROWS = HEAD_DIM + CHUNK
VMEM_LIMIT = 56 * 1024 * 1024

GATE_KINDS = GATE_BETA, GATE_G, GATE_I, GATE_F = tuple(range(4))

COL_GQ, COL_GK, COL_GV, COL_GZ = 32, 40, 48, 56
COL_MQ, COL_MK, COL_MV, COL_MO, COL_MZ = 64, 72, 80, 88, 96
BIG_COLS = 104 * LANES

_NT_DIMS = (((1,), (1,)), ((), ()))


def _mm(a, b):
    return jnp.dot(a.astype(BF16), b.astype(BF16), preferred_element_type=F32)


def _mm_nt(a, b):
    return lax.dot_general(a.astype(BF16), b.astype(BF16), _NT_DIMS, preferred_element_type=F32)


def _sigmoid(x):
    return 0.5 * jnp.tanh(0.5 * x) + 0.5


def _silu(x):
    h = 0.5 * x
    return h * jnp.tanh(h) + h


def _in_proj_kernel(x_ref, nw_ref, wbig_ref, wsmall_ref, gpar_ref, big_ref, gates_ref, n_scr):
    @pl.when(pl.program_id(1) == 0)
    def _():
        x = x_ref[...]
        y = x * lax.rsqrt(jnp.mean(x * x, axis=-1, keepdims=True) + NORM_EPS) * nw_ref[...]
        n_scr[...] = y.astype(BF16)
        pre = lax.dot_general(n_scr[...], wsmall_ref[...].astype(BF16), _NT_DIMS, preferred_element_type=F32)
        z = pre + gpar_ref[1:2, :]
        col = lax.broadcasted_iota(jnp.int32, z.shape, 1)
        per_head = len(GATE_KINDS) * N_DIR
        kind = lax.shift_right_logical(jnp.bitwise_and(col, per_head - 1), 1)
        log_decay = -jnp.exp(gpar_ref[0:1, :]) * jax.nn.softplus(z)
        act = jnp.where(kind == GATE_BETA, jax.nn.sigmoid(z),
                        jnp.where(kind == GATE_G, log_decay,
                                  jnp.where(kind == GATE_I, z, jax.nn.log_sigmoid(z))))
        for h in range(HEADS):
            gates_ref[h] = act[:, h * per_head:(h + 1) * per_head]

    big_ref[...] = lax.dot_general(n_scr[...], wbig_ref[...].astype(BF16), _NT_DIMS,
                                   preferred_element_type=F32).astype(big_ref.dtype)


def _in_proj(x2, norm_w, wt, wt_small, gate_params, row_runs, t):
    m, d = x2.shape
    tm = min(1024, t)
    tn = 1024
    per_head = len(GATE_KINDS) * N_DIR
    blocks_per_seq = t // tm
    assert N_DIR == 2 and t % tm == 0
    assert all(col % tn == 0 and row % BF16_SUBLANES == 0 for col, row in row_runs)

    def source_row(j):
        row = jnp.int32(0)
        for col, first_row in row_runs:
            row = jnp.where(j >= col // tn, first_row + (j - col // tn) * tn, row)
        return pl.multiple_of(row, BF16_SUBLANES)

    return pl.pallas_call(
        _in_proj_kernel,
        grid=(m // tm, BIG_COLS // tn),
        in_specs=[
            pl.BlockSpec((tm, d), lambda i, j: (i, 0)),
            pl.BlockSpec((1, d), lambda i, j: (0, 0)),
            pl.BlockSpec((pl.Element(tn), pl.Element(d)), lambda i, j: (source_row(j), 0)),
            pl.BlockSpec((LANES, d), lambda i, j: (0, 0)),
            pl.BlockSpec((2, LANES), lambda i, j: (0, 0)),
        ],
        out_specs=[
            pl.BlockSpec((tm, tn), lambda i, j: (i, j)),
            pl.BlockSpec((None, HEADS, tm, per_head),
                         lambda i, j: (i // blocks_per_seq, 0, i % blocks_per_seq, 0)),
        ],
        out_shape=[
            jax.ShapeDtypeStruct((m, BIG_COLS), BF16),
            jax.ShapeDtypeStruct((m // t, HEADS, t, per_head), F32),
        ],
        scratch_shapes=[pltpu.VMEM((tm, d), BF16)],
        compiler_params=pltpu.CompilerParams(
            dimension_semantics=("parallel", "arbitrary"), vmem_limit_bytes=VMEM_LIMIT),
        name="in_proj",
    )(x2, norm_w, wt, wt_small, gate_params)


def _conv_silu_into(src_ref, w_ref, pad_scr, dst_scr, post):
    t = src_ref.shape[0]
    rb = min(CONV_ROWS, t)
    zeros = jnp.zeros((CONV_HALO, HEAD_DIM), F32)
    pad_scr[0:CONV_HALO, :] = zeros
    pad_scr[t + CONV_HALO:t + 2 * CONV_HALO, :] = zeros

    def copy(r, carry):
        off = pl.multiple_of(r * rb, rb)
        pad_scr[pl.ds(off + CONV_HALO, rb), :] = src_ref[pl.ds(off, rb), :].astype(F32)
        return carry

    lax.fori_loop(0, t // rb, copy, 0)
    w = w_ref[...]
    pad = (CONV_WIDTH - 1) // 2

    def body(r, carry):
        off = pl.multiple_of(r * rb, rb)
        acc = None
        for j in range(CONV_WIDTH):
            term = pad_scr[pl.ds(off + (CONV_HALO - pad + j), rb), :] * w[j:j + 1, :]
            acc = term if acc is None else acc + term
        dst_scr[pl.ds(off, rb), :] = post(_silu(acc)).astype(dst_scr.dtype)
        return carry

    lax.fori_loop(0, t // rb, body, 0)


def _direction_masks():
    row = lax.broadcasted_iota(jnp.int32, (CHUNK, CHUNK), 0)
    col = lax.broadcasted_iota(jnp.int32, (CHUNK, CHUNK), 1)
    fwd = (row >= col, row > col)
    bwd = (row <= col, row < col)
    out = []
    for (incl, strict), last in ((fwd, CHUNK - 1), (bwd, 0)):
        tri = incl.astype(BF16)
        out.append((incl, strict, jnp.concatenate([tri, tri, tri], axis=1), last))
    return out


def _cumsum_pieces(gates, kind, d):
    c = N_DIR * kind + d
    x = jnp.broadcast_to(gates[:, c:c + 1], (CHUNK, LANES))
    hi = x.astype(BF16)
    rest = x - hi.astype(F32)
    mid = rest.astype(BF16)
    lo = (rest - mid.astype(F32)).astype(BF16)
    return jnp.concatenate([hi, mid, lo], axis=0)


def _as_rows(col_bcast):
    return col_bcast.T[:CHUNK, :]


def _each(fn, *columns):
    return [fn(*args) for args in zip(*columns)]


def _interleave(*generators):
    live = list(generators)
    while live:
        for gen in list(live):
            try:
                next(gen)
            except StopIteration:
                live.remove(gen)


def _wide_index():
    row = lax.broadcasted_iota(jnp.int32, (CHUNK, 2 * CHUNK), 0)
    lane = lax.broadcasted_iota(jnp.int32, (CHUNK, 2 * CHUNK), 1)
    return row, jnp.bitwise_and(lane, CHUNK - 1), lane >= CHUNK


def _pair_masks():
    row, col, right = _wide_index()
    masks = []
    log_size = 0
    while (1 << log_size) < CHUNK:
        block = lambda x, bits: lax.shift_right_logical(x, bits)
        same_pair = block(row, log_size + 1) == block(col, log_size + 1)
        halves = block(row, log_size) != block(col, log_size)
        masks.append(jnp.logical_and(right, jnp.logical_and(same_pair, halves)))
        log_size += 1
    return masks


def _unit_triangular_inverses(lms_wide, pair_masks):
    row, col, right = _wide_index()
    identity = jnp.logical_and(jnp.logical_not(right), row == col).astype(F32)
    packed = [jnp.where(right, lm, identity) for lm in lms_wide]
    zero_rows = jnp.zeros((CHUNK, 2 * CHUNK), BF16)
    for mask in pair_masks:
        def level(p):
            joined = jnp.where(mask, p, 0.0).astype(BF16)
            return jnp.dot(joined, jnp.concatenate([zero_rows, p.astype(BF16)], axis=0),
                           preferred_element_type=F32)

        prods = _each(level, packed)
        yield
        packed = _each(lambda p, d: p - d, packed, prods)
    return [p[:, :CHUNK] for p in packed]


def _mixers_kernel(gq_ref, gk_ref, gv_ref, mq_ref, mk_ref, mv_ref, g_ref,
                   cgq_ref, cgk_ref, cgv_ref, cmq_ref, cmk_ref,
                   ya_ref, yb_ref,
                   pad_scr, gq_scr, gk_scr, gv_scr, go_scr, mq_scr, mk_scr, mo_scr,
                   aq_pend, b_pend, o_pend, gam_pend, cf_scr, cb_scr):
    t = gq_ref.shape[0]
    n_chunks = t // CHUNK
    group = min(CHUNK_GROUP, n_chunks)
    n_groups = n_chunks // group

    def l2(y, scale_sq=1.0):
        total = jnp.sum(y * y, axis=-1, keepdims=True) + NORM_EPS
        return y * lax.rsqrt(total if scale_sq == 1.0 else total * (1.0 / scale_sq))

    _conv_silu_into(gq_ref, cgq_ref, pad_scr, gq_scr, lambda y: l2(y, 1.0 / HEAD_DIM))
    _conv_silu_into(gk_ref, cgk_ref, pad_scr, gk_scr, l2)
    _conv_silu_into(gv_ref, cgv_ref, pad_scr, gv_scr, lambda y: y)
    _conv_silu_into(mq_ref, cmq_ref, pad_scr, mq_scr, lambda y: y)
    _conv_silu_into(mk_ref, cmk_ref, pad_scr, mk_scr, lambda y: y * (HEAD_DIM ** -0.5))

    go_scr[...] = jnp.zeros_like(go_scr)
    mo_scr[...] = jnp.zeros_like(mo_scr)
    cf_scr[...] = jnp.zeros_like(cf_scr)
    cb_scr[...] = jnp.zeros_like(cb_scr)

    masks = _direction_masks()
    pair_masks = _pair_masks()
    wide_row, wide_col, _ = _wide_index()
    wide_masks = [(wide_row >= wide_col, wide_row > wide_col), (wide_row <= wide_col, wide_row < wide_col)]
    ones_col = (lax.broadcasted_iota(jnp.int32, (CHUNK, LANES), 1) == 0).astype(BF16)

    def chains_of(i):
        chains = []
        for j in range(group):
            chains += [(i * group + j, 0), (n_chunks - 1 - (i * group + j), 1)]
        return chains

    def chunk_cumsum(gates, kind, d):
        return jnp.dot(masks[d][2], _cumsum_pieces(gates, kind, d), preferred_element_type=F32)


    def gdn_terms(i):
        chains = chains_of(i)
        dirs = [d for _, d in chains]
        offs = [pl.multiple_of(c * CHUNK, CHUNK) for c, _ in chains]
        q_bf = [gq_scr[pl.ds(off, CHUNK), :] for off in offs]
        k_bf = [gk_scr[pl.ds(off, CHUNK), :] for off in offs]
        kq = _each(lambda k, q: _mm_nt(jnp.concatenate([k, q], axis=0), jnp.concatenate([k, k], axis=0)),
                   k_bf, q_bf)
        yield
        gt = [g_ref[pl.ds(off, CHUNK), :] for off in offs]
        gcb = _each(lambda x, d: chunk_cumsum(x, GATE_G, d), gt, dirs)
        yield
        qc = [x.astype(F32) for x in q_bf]
        kc = [x.astype(F32) for x in k_bf]
        vc = [gv_scr[pl.ds(off, CHUNK), :].astype(F32) for off in offs]
        beta = _each(lambda x, d: x[:, N_DIR * GATE_BETA + d:N_DIR * GATE_BETA + d + 1], gt, dirs)
        gc_rows = _each(lambda g: jnp.concatenate([g, g], axis=0).T[:CHUNK, :], gcb)
        decay = _each(lambda g, r, d: jnp.exp(jnp.where(wide_masks[d][0], g - r, -jnp.inf)), gcb, gc_rows, dirs)
        lm = _each(lambda x, b, dec, d: jnp.where(wide_masks[d][1], x[:CHUNK] * b * dec, 0.0),
                   kq, beta, decay, dirs)
        attn = _each(lambda x, dec: x[CHUNK:, :CHUNK] * dec[:, :CHUNK], kq, decay)
        egc = _each(jnp.exp, gcb)
        rhs = _each(lambda v, k, b, e: jnp.concatenate([v * b, k * b * e], axis=1), vc, kc, beta, egc)
        inv = yield from _unit_triangular_inverses(lm, pair_masks)
        sol = _each(_mm, inv, rhs)
        yield
        g_last = _each(lambda g, d: g[masks[d][3]:masks[d][3] + 1, :], gcb, dirs)
        k_dec_t = _each(lambda k, gl, g: (k * jnp.exp(gl - g)).T, kc, g_last, gcb)
        prod = _each(lambda a, kt, s: _mm(jnp.concatenate([a, kt], axis=0), s), attn, k_dec_t, sol)
        yield
        for slot in range(len(chains)):
            p = prod[slot]
            q_eff = qc[slot] * egc[slot] - p[:CHUNK, HEAD_DIM:]
            aq_pend[slot * AQ_ROWS:(slot + 1) * AQ_ROWS, :] = jnp.concatenate(
                [p[CHUNK:, HEAD_DIM:], q_eff], axis=0).astype(BF16)
            b_pend[slot * HEAD_DIM:(slot + 1) * HEAD_DIM, :] = p[CHUNK:, :HEAD_DIM]
            o_pend[slot * CHUNK:(slot + 1) * CHUNK, :] = p[:CHUNK, :HEAD_DIM]
            gam_pend[slot * 8:(slot + 1) * 8, :] = jnp.broadcast_to(jnp.exp(g_last[slot]), (8, LANES))

    def gdn_advance(state, slot, chunk):
        off = pl.multiple_of(chunk * CHUNK, CHUNK)
        r = jnp.dot(aq_pend[slot * AQ_ROWS:(slot + 1) * AQ_ROWS, :], state.astype(BF16),
                    preferred_element_type=F32)
        go_scr[pl.ds(off, CHUNK), :] += o_pend[slot * CHUNK:(slot + 1) * CHUNK, :] + r[HEAD_DIM:]
        return (state * gam_pend[slot * 8:slot * 8 + 1, :] - r[:HEAD_DIM]
                + b_pend[slot * HEAD_DIM:(slot + 1) * HEAD_DIM, :])

    def gdn_walk(i, states):
        for j in range(group):
            states[0] = gdn_advance(states[0], 2 * j, i * group + j)
            states[1] = gdn_advance(states[1], 2 * j + 1, n_chunks - 1 - (i * group + j))
            yield


    def mlstm_advance(state, terms):
        c_st, n_st, m_st = state
        qc, bcb, m_intra, b_last, m_loc, delta_c, delta_n, intra, off = terms
        m_inter = bcb + m_st
        m_t = jnp.maximum(m_inter, m_intra)
        w_inter = jnp.exp(m_inter - m_t)
        w_intra = jnp.exp(m_intra - m_t)
        inter = _mm(qc, c_st)
        num = w_inter * inter + w_intra * intra[:, :HEAD_DIM]
        den = (w_inter[:, :1] * jnp.sum(qc.astype(F32) * n_st, axis=1, keepdims=True)
               + w_intra[:, :1] * intra[:, HEAD_DIM:HEAD_DIM + 1])
        mo_scr[pl.ds(off, CHUNK), :] += num / jnp.maximum(jnp.abs(den), jnp.exp(-m_t))
        m_new = jnp.maximum(b_last + m_st, m_loc)
        s_old = jnp.exp(b_last + m_st - m_new)
        s_loc = jnp.exp(m_loc - m_new)
        return c_st * s_old + delta_c * s_loc, n_st * s_old + delta_n * s_loc, m_new

    def mlstm_group(i, rows):
        chains = chains_of(i)
        dirs = [d for _, d in chains]
        offs = [pl.multiple_of(c * CHUNK, CHUNK) for c, _ in chains]
        qc = [mq_scr[pl.ds(off, CHUNK), :] for off in offs]
        kc = [mk_scr[pl.ds(off, CHUNK), :] for off in offs]
        v_aug = [jnp.concatenate([mv_ref[pl.ds(off, CHUNK), :], ones_col], axis=1) for off in offs]
        gt = [g_ref[pl.ds(off, CHUNK), :] for off in offs]
        qk = _each(_mm_nt, qc, kc)
        yield
        igb = _each(lambda x, d: jnp.broadcast_to(
            x[:, N_DIR * GATE_I + d:N_DIR * GATE_I + d + 1], (CHUNK, LANES)), gt, dirs)
        bcb = _each(lambda x, d: chunk_cumsum(x, GATE_F, d), gt, dirs)
        yield
        bc_rows = _each(_as_rows, bcb)
        ig_rows = _each(_as_rows, igb)
        d_log = _each(lambda b, br, ir, d: jnp.where(masks[d][0], b[:, :CHUNK] - br + ir, -jnp.inf),
                      bcb, bc_rows, ig_rows, dirs)
        m_intra = _each(lambda x: jnp.max(x, axis=1, keepdims=True), d_log)
        b_last = _each(lambda b, d: b[masks[d][3]:masks[d][3] + 1, :], bcb, dirs)
        a_end = _each(lambda bl, b, ig: bl - b + ig, b_last, bcb, igb)
        m_loc = _each(lambda x: jnp.max(x, axis=0, keepdims=True), a_end)
        k_end = _each(lambda k, a, m: k.astype(F32) * jnp.exp(a - m), kc, a_end, m_loc)
        delta_n = _each(lambda x: jnp.sum(x, axis=0, keepdims=True), k_end)
        delta_c = _each(lambda x, v: _mm(x.T, v[:, :HEAD_DIM]), k_end, v_aug)
        yield
        intra = _each(lambda s, dl, m, v: _mm(s * jnp.exp(dl - m), v), qk, d_log, m_intra, v_aug)
        yield
        terms = [(qc[k], bcb[k], m_intra[k], b_last[k], m_loc[k], delta_c[k], delta_n[k], intra[k], offs[k])
                 for k in range(len(chains))]
        s_fwd = (cf_scr[...], rows[0], rows[1])
        s_bwd = (cb_scr[...], rows[2], rows[3])
        for j in range(group):
            s_fwd = mlstm_advance(s_fwd, terms[2 * j])
            s_bwd = mlstm_advance(s_bwd, terms[2 * j + 1])
            yield
        cf_scr[...] = s_fwd[0]
        cb_scr[...] = s_bwd[0]
        rows[:] = [s_fwd[1], s_fwd[2], s_bwd[1], s_bwd[2]]


    _interleave(gdn_terms(0))

    def step(i, carry):
        states = list(carry[:2])
        rows = list(carry[2:])
        _interleave(gdn_terms(i), gdn_walk(i - 1, states), mlstm_group(i - 1, rows))
        return tuple(states) + tuple(rows)

    zero_state = jnp.zeros((HEAD_DIM, HEAD_DIM), F32)
    zero_row = jnp.zeros((1, LANES), F32)
    carry = lax.fori_loop(1, n_groups, step, (zero_state, zero_state) + (zero_row,) * 4)
    states = list(carry[:2])
    rows = list(carry[2:])
    _interleave(gdn_walk(n_groups - 1, states), mlstm_group(n_groups - 1, rows))


    rb = min(CONV_ROWS, t)

    def epilogue(r, carry):
        off = pl.multiple_of(r * rb, rb)
        ya_ref[pl.ds(off, rb), :] = go_scr[pl.ds(off, rb), :].astype(ya_ref.dtype)
        yb_ref[pl.ds(off, rb), :] = mo_scr[pl.ds(off, rb), :].astype(yb_ref.dtype)
        return carry

    lax.fori_loop(0, t // rb, epilogue, 0)


def _mixers(proj_big, gsm, conv_gdn, conv_mlstm, batch, t):
    slots = N_DIR * min(CHUNK_GROUP, t // CHUNK)
    head_block = lambda base: pl.BlockSpec((t, HEAD_DIM), lambda b, h: (b, base + h))
    conv_block = lambda base: pl.BlockSpec((CONV_WIDTH, HEAD_DIM), lambda b, h: (0, base + h))
    seq = lambda dtype: pltpu.VMEM((t, HEAD_DIM), dtype)
    out_spec = pl.BlockSpec((t, HEAD_DIM), lambda b, h: (b, h))
    out_shape = jax.ShapeDtypeStruct((batch * t, HEADS * HEAD_DIM), BF16)
    return pl.pallas_call(
        _mixers_kernel,
        grid=(batch, HEADS),
        in_specs=[
            head_block(COL_GQ), head_block(COL_GK), head_block(COL_GV),
            head_block(COL_MQ), head_block(COL_MK), head_block(COL_MV),
            pl.BlockSpec((None, None, t, len(GATE_KINDS) * N_DIR), lambda b, h: (b, h, 0, 0)),
            conv_block(0), conv_block(HEADS), conv_block(2 * HEADS),
            conv_block(0), conv_block(HEADS),
        ],
        out_specs=[out_spec, out_spec],
        out_shape=[out_shape, out_shape],
        scratch_shapes=[
            pltpu.VMEM((t + 2 * CONV_HALO, HEAD_DIM), F32),
            seq(BF16), seq(BF16), seq(BF16), seq(F32),
            seq(BF16), seq(BF16), seq(F32),
            pltpu.VMEM((slots * AQ_ROWS, HEAD_DIM), BF16),
            pltpu.VMEM((slots * HEAD_DIM, HEAD_DIM), F32),
            pltpu.VMEM((slots * CHUNK, HEAD_DIM), F32),
            pltpu.VMEM((slots * 8, LANES), F32),
            pltpu.VMEM((HEAD_DIM, HEAD_DIM), F32),
            pltpu.VMEM((HEAD_DIM, HEAD_DIM), F32),
        ],
        compiler_params=pltpu.CompilerParams(
            dimension_semantics=("parallel", "parallel"), vmem_limit_bytes=VMEM_LIMIT),
        name="mixers",
    )(proj_big, proj_big, proj_big, proj_big, proj_big, proj_big, gsm,
      conv_gdn, conv_gdn, conv_gdn, conv_mlstm, conv_mlstm)


def _out_proj_kernel(oa_ref, ob_ref, gz_ref, mo_ref, mz_ref, ga_ref, gb_ref, x_ref, wa_ref, wb_ref, wo_ref,
                     gnw_ref, mnw_ref, gbias_ref, fw_ref, o_ref):
    d = x_ref.shape[1]
    gbias = gbias_ref[...]
    gnw = gnw_ref[...]
    ya_heads, yb_heads = [], []
    for h in range(HEADS):
        cols = slice(h * HEAD_DIM, (h + 1) * HEAD_DIM)
        o = oa_ref[:, cols].astype(F32)
        o = o * lax.rsqrt(jnp.mean(o * o, axis=-1, keepdims=True) + NORM_EPS) * gnw
        ya_heads.append((o * _silu(gz_ref[:, cols].astype(F32))).astype(BF16))
        hh = ob_ref[:, cols].astype(F32)
        hc = hh - jnp.mean(hh, axis=-1, keepdims=True)
        hn = hc * lax.rsqrt(jnp.mean(hc * hc, axis=-1, keepdims=True) + NORM_EPS) * mnw_ref[:, cols]
        gate = _sigmoid(mo_ref[:, cols].astype(F32)) * _silu(mz_ref[:, cols].astype(F32))
        yb_heads.append((hn * gate).astype(BF16))
    a = jnp.dot(jnp.concatenate(ya_heads, axis=1), wa_ref[...].astype(BF16), preferred_element_type=F32)
    b = jnp.dot(jnp.concatenate(yb_heads, axis=1), wb_ref[...].astype(BF16), preferred_element_type=F32)
    merged = (jax.nn.sigmoid(ga_ref[...].astype(F32) + gbias[:, :d]) * a
              + jax.nn.sigmoid(gb_ref[...].astype(F32) + gbias[:, d:]) * b)
    xo = x_ref[...] + jnp.dot(merged.astype(BF16), wo_ref[...].astype(BF16), preferred_element_type=F32)
    o_ref[...] = xo * lax.rsqrt(jnp.mean(xo * xo, axis=-1, keepdims=True) + NORM_EPS) * fw_ref[...]


def _out_proj(ya, yb, proj_big, x2, wa, wb, wo, gdn_norm_w, mlstm_norm_w, gate_bias, final_w):
    m, d = x2.shape
    c = ya.shape[1]
    col_block = lambda base: base * HEAD_DIM // c
    assert all((base * HEAD_DIM) % c == 0 for base in (COL_GZ, COL_MO, COL_MZ))
    tm = min(256, m)
    const = lambda shape: pl.BlockSpec(shape, lambda i: (0, 0))
    single = lambda shape: pl.BlockSpec(shape, lambda i: (0, 0), pipeline_mode=pl.Buffered(1))
    return pl.pallas_call(
        _out_proj_kernel,
        grid=(m // tm,),
        in_specs=[
            pl.BlockSpec((tm, c), lambda i: (i, 0)),
            pl.BlockSpec((tm, c), lambda i: (i, 0)),
            pl.BlockSpec((tm, c), lambda i: (i, col_block(COL_GZ))),
            pl.BlockSpec((tm, c), lambda i: (i, col_block(COL_MO))),
            pl.BlockSpec((tm, c), lambda i: (i, col_block(COL_MZ))),
            pl.BlockSpec((tm, d), lambda i: (i, 0)),
            pl.BlockSpec((tm, d), lambda i: (i, 1)),
            pl.BlockSpec((tm, d), lambda i: (i, 0)),
            single((c, d)), single((c, d)), single((d, d)),
            const((1, HEAD_DIM)), const((1, c)),
            const((1, 2 * d)), const((1, d)),
        ],
        out_specs=pl.BlockSpec((tm, d), lambda i: (i, 0)),
        out_shape=jax.ShapeDtypeStruct((m, d), F32),
        compiler_params=pltpu.CompilerParams(
            dimension_semantics=("parallel",), vmem_limit_bytes=VMEM_LIMIT),
        name="out_proj",
    )(ya, yb, proj_big, proj_big, proj_big, proj_big, proj_big, x2, wa, wb, wo, gdn_norm_w, mlstm_norm_w,
      gate_bias, final_w)


def kernel(x, w_in, conv_gdn, gdn_a_log, gdn_dt_bias, gdn_norm_w, conv_mlstm, mlstm_i_bias, mlstm_f_bias,
           mlstm_norm_w, gate_bias, w_branch_gdn, w_branch_mlstm, w_out, norm_w, final_norm_w):
    batch, t, d = x.shape
    depth = w_in.shape[0]
    assert depth == 1, "the final rmsnorm is fused into the single layer's output projection"
    key = HEADS * HEAD_DIM
    sizes = (3 * key, key, N_DIR * HEADS, N_DIR * HEADS, 2 * key, key, key, key, N_DIR * HEADS, N_DIR * HEADS,
             2 * d)
    offs = [0]
    for s in sizes:
        offs.append(offs[-1] + s)
    seg = lambda w, i: w[offs[i]:offs[i + 1], :]

    x2 = x.reshape(batch * t, d)
    for layer in range(depth):
        wt = jnp.swapaxes(w_in[layer], 0, 1)
        head_major = lambda a: a.reshape((len(GATE_KINDS), N_DIR, HEADS) + a.shape[1:]).transpose(
            (2, 0, 1) + tuple(range(3, a.ndim + 2))).reshape(a.shape)
        wt_small = head_major(jnp.concatenate([seg(wt, 2), seg(wt, 3), seg(wt, 8), seg(wt, 9)], axis=0))
        wt_small = jnp.pad(wt_small, ((0, LANES - wt_small.shape[0]), (0, 0)))
        row_runs = ((0, offs[10]), (2 * d, offs[0]), (2 * d + 4 * key, offs[4]))
        width = N_DIR * HEADS
        zeros = jnp.zeros((width,), F32)
        gate_params = jnp.stack([
            head_major(jnp.concatenate([zeros, gdn_a_log[layer].reshape(-1), zeros, zeros])),
            head_major(jnp.concatenate([zeros, gdn_dt_bias[layer].reshape(-1),
                                        mlstm_i_bias[layer].reshape(-1), mlstm_f_bias[layer].reshape(-1)]))])
        gate_params = jnp.pad(gate_params, ((0, 0), (0, LANES - 4 * width)))
        proj_big, gates = _in_proj(x2, norm_w[layer][None, :], wt, wt_small, gate_params, row_runs, t)
        ya, yb = _mixers(proj_big, gates, conv_gdn[layer], conv_mlstm[layer], batch, t)
        x2 = _out_proj(ya, yb, proj_big, x2, w_branch_gdn[layer], w_branch_mlstm[layer], w_out[layer],
                       gdn_norm_w[layer][None, :], mlstm_norm_w[layer][None, :],
                       gate_bias[layer][None, :], final_norm_w[None, :])
    return x2.reshape(batch, t, d)
```
